```python
import math
import jax, jax.numpy as jnp
from jax import lax
import numpy as np

D_MODEL = 4096
BATCH = 4
SEQ = 2048
DEPTH = 2
DEC_BATCH = 32
DEC_SEQ = 32
PAST_LEN = 4096

CHUNK = 64
N_MIXERS = 2
N_SSD_LAYERS = (DEPTH + 1) // 2
N_SB_LAYERS = DEPTH // 2
N_DENSE_LAYERS = (DEPTH + 1) // 2
N_MOE_LAYERS = DEPTH // 2
EPS = 1e-6

SSD_EXPAND = 2
SSD_INNER = SSD_EXPAND * D_MODEL
SSD_HEAD_DIM = 64
SSD_HEADS = SSD_INNER // SSD_HEAD_DIM
SSD_GROUPS = 8
SSD_HEADS_PER_GROUP = SSD_HEADS // SSD_GROUPS
SSD_STATE = 128
SSD_CONV = 4
SSD_GN = SSD_GROUPS * SSD_STATE
SSD_CONV_DIM = SSD_INNER + 2 * SSD_GN
SSD_IN_DIM = SSD_INNER + SSD_CONV_DIM + SSD_HEADS
SSD_CHUNK = CHUNK
DT_MIN = 1e-3
DT_MAX = 1e-1

SB_HEADS = 32
SB_HEAD_DIM = D_MODEL // SB_HEADS
SB_BLOCK = 128
SB_SCALE = SB_HEAD_DIM ** -0.5

D_FF = 7 * D_MODEL // 2
N_EXPERTS = 8
TOP_K = 2

kernel_name = 'hybrid_ssd_stickbreak_adaln_stream_step'


def _rms(x):
    xf = x.astype(jnp.float32)
    return (xf * lax.rsqrt(jnp.mean(xf * xf, axis=-1, keepdims=True) + EPS)).astype(x.dtype)


def _modulate(x, shift, scale):
    return _rms(x) * (1 + scale[:, None, :]) + shift[:, None, :]


def _ssd_scan(x, dt, a, b_in, c_in, s0):
    bsz, L = x.shape[0], x.shape[1]
    q = SSD_CHUNK if L % SSD_CHUNK == 0 else L
    nc = L // q
    f32 = jnp.float32
    G, R, P, N = SSD_GROUPS, SSD_HEADS_PER_GROUP, SSD_HEAD_DIM, SSD_STATE
    xdt = (x.astype(f32) * dt[..., None]).reshape(bsz, nc, q, G, R, P)
    bc = b_in.astype(f32).reshape(bsz, nc, q, G, N)
    cc = c_in.astype(f32).reshape(bsz, nc, q, G, N)
    a_cs = jnp.cumsum((dt * a).reshape(bsz, nc, q, G, R), axis=2)
    causal = jnp.tril(jnp.ones((q, q), bool))[None, None, :, :, None, None]
    seg = a_cs[:, :, :, None] - a_cs[:, :, None, :]
    decay = jnp.exp(jnp.where(causal, seg, -jnp.inf))
    cb = jnp.einsum('bclgn,bcsgn->bclsg', cc, bc)
    y_diag = jnp.einsum('bclsg,bclsgr,bcsgrp->bclgrp', cb, decay, xdt)
    to_end = jnp.exp(a_cs[:, :, -1:] - a_cs)
    chunk_states = jnp.einsum('bcsgn,bcsgr,bcsgrp->bcgrpn', bc, to_end, xdt)
    chunk_decay = jnp.exp(a_cs[:, :, -1])

    def step(s, inp):
        dec, st = inp
        return s * dec[..., None, None] + st, s

    s_final, s_prev = lax.scan(step, s0.astype(f32),
                               (jnp.moveaxis(chunk_decay, 1, 0), jnp.moveaxis(chunk_states, 1, 0)))
    s_prev = jnp.moveaxis(s_prev, 0, 1)
    y_off = jnp.einsum('bclgn,bcgrpn,bclgr->bclgrp', cc, s_prev, jnp.exp(a_cs))
    y = (y_diag + y_off).reshape(bsz, L, G, R, P)
    return y, s_final


def _ssd_mixer(h, conv_prev, ssm_prev, w_in, conv_w, conv_b, dt_bias, a_log, d_skip, norm_w, w_out):
    bsz, L, _ = h.shape
    G, R, P, N = SSD_GROUPS, SSD_HEADS_PER_GROUP, SSD_HEAD_DIM, SSD_STATE
    proj = h @ w_in
    z = proj[..., :SSD_INNER]
    xbc = proj[..., SSD_INNER:SSD_INNER + SSD_CONV_DIM]
    dt_raw = proj[..., SSD_INNER + SSD_CONV_DIM:]
    xpad = jnp.concatenate([conv_prev.astype(xbc.dtype), xbc], axis=1)
    conv = conv_b + sum(xpad[:, k:k + L] * conv_w[k] for k in range(SSD_CONV))
    xbc = jax.nn.silu(conv)
    x = xbc[..., :SSD_INNER].reshape(bsz, L, G, R, P)
    b_in = xbc[..., SSD_INNER:SSD_INNER + SSD_GN].reshape(bsz, L, G, N)
    c_in = xbc[..., SSD_INNER + SSD_GN:].reshape(bsz, L, G, N)
    dt = jax.nn.softplus(dt_raw.astype(jnp.float32) + dt_bias.astype(jnp.float32)).reshape(bsz, L, G, R)
    a = -jnp.exp(a_log.astype(jnp.float32)).reshape(G, R)
    y, s_final = _ssd_scan(x, dt, a, b_in, c_in, ssm_prev.reshape(bsz, G, R, P, N))
    y = y + d_skip.astype(jnp.float32).reshape(G, R)[:, :, None] * x.astype(jnp.float32)
    g = (y.reshape(bsz, L, SSD_INNER) * jax.nn.silu(z.astype(jnp.float32))).reshape(bsz, L, G, SSD_INNER // G)
    g = g * lax.rsqrt(jnp.mean(g * g, axis=-1, keepdims=True) + EPS)
    g = (g.reshape(bsz, L, SSD_INNER) * norm_w).astype(h.dtype)
    new_conv = xpad[:, L:]
    new_ssm = s_final.reshape(bsz, SSD_HEADS, P, N).astype(ssm_prev.dtype)
    return g @ w_out, new_conv, new_ssm


def _stick_breaking(q, k_segs, v_segs, q_pos, k_pos):
    z = jnp.concatenate([jnp.einsum('bqhd,bkhd->bhqk', q, k) for k in k_segs], axis=-1)
    z = z.astype(jnp.float32) * SB_SCALE
    valid = k_pos[None, :] < q_pos[:, None]
    log_keep = jnp.where(valid, jax.nn.log_sigmoid(-z), 0.0)
    later = lax.cumsum(log_keep, axis=3, reverse=True) - log_keep
    w = jnp.where(valid, jnp.exp(jax.nn.log_sigmoid(z) + later), 0.0)
    out = 0
    off = 0
    for v in v_segs:
        n = v.shape[1]
        out = out + jnp.einsum('bhqk,bkhd->bqhd', w[..., off:off + n].astype(v.dtype), v)
        off += n
    return out


def _sb_mixer(h, k_cache, v_cache, w_qkv, w_o):
    bsz, L, _ = h.shape
    qkv = (h @ w_qkv).reshape(bsz, L, 3, SB_HEADS, SB_HEAD_DIM)
    q, k, v = qkv[:, :, 0], qkv[:, :, 1], qkv[:, :, 2]
    if k_cache is None:
        nb = L // SB_BLOCK
        qb = q.reshape(bsz, nb, SB_BLOCK, SB_HEADS, SB_HEAD_DIM).swapaxes(0, 1)
        k_pos = jnp.arange(L)

        def block(args):
            qi, bi = args
            q_pos = bi * SB_BLOCK + jnp.arange(SB_BLOCK)
            return _stick_breaking(qi, [k], [v], q_pos, k_pos)

        o = lax.map(block, (qb, jnp.arange(nb))).swapaxes(0, 1)
    else:
        past = k_cache.shape[1]
        q_pos = past + jnp.arange(L)
        k_pos = jnp.arange(past + L)
        o = _stick_breaking(q, [k_cache.astype(k.dtype), k], [v_cache.astype(v.dtype), v], q_pos, k_pos)
    o = o.reshape(bsz, L, SB_HEADS * SB_HEAD_DIM)
    return o @ w_o, k, v


def _swiglu(h, w_gate, w_up, w_down):
    return (jax.nn.silu(h @ w_gate) * (h @ w_up)) @ w_down


def _moe(h, w_router, w_gate, w_up, w_down):
    logits = (h @ w_router).astype(jnp.float32)
    top_val, top_idx = lax.top_k(logits, TOP_K)
    gates = jax.nn.softmax(top_val, axis=-1)
    combine = jnp.sum(jax.nn.one_hot(top_idx, N_EXPERTS, dtype=jnp.float32) * gates[..., None], axis=-2)
    out = jnp.zeros_like(h)
    for e in range(N_EXPERTS):
        out = out + combine[..., e:e + 1].astype(h.dtype) * _swiglu(h, w_gate[e], w_up[e], w_down[e])
    return out


def _trunk(x, c, ssm0, conv0, k_cache, v_cache, p):
    cs = jax.nn.silu(c)
    ssm_out, conv_out, k_out, v_out = [], [], [], []
    for i in range(DEPTH):
        j = i // N_MIXERS
        mod = cs @ p['w_mod'][i] + p['b_mod'][i]
        sh1, sc1, g1, sh2, sc2, g2 = jnp.split(mod, 6, axis=-1)
        h = _modulate(x, sh1, sc1)
        if i % N_MIXERS == 0:
            o, conv_new, ssm_new = _ssd_mixer(
                h, conv0[j], ssm0[j], p['ssd_w_in'][j], p['ssd_conv_w'][j], p['ssd_conv_b'][j],
                p['ssd_dt_bias'][j], p['ssd_a_log'][j], p['ssd_d'][j], p['ssd_norm_w'][j], p['ssd_w_out'][j])
            conv_out.append(conv_new)
            ssm_out.append(ssm_new)
        else:
            kc = None if k_cache is None else k_cache[j]
            vc = None if v_cache is None else v_cache[j]
            o, k_new, v_new = _sb_mixer(h, kc, vc, p['sb_w_qkv'][j], p['sb_w_o'][j])
            k_out.append(k_new)
            v_out.append(v_new)
        x = x + g1[:, None, :] * o
        h = _modulate(x, sh2, sc2)
        jc = i // 2
        if i % 2 == 0:
            f = _swiglu(h, p['ffn_w_gate'][jc], p['ffn_w_up'][jc], p['ffn_w_down'][jc])
        else:
            f = _moe(h, p['moe_w_router'][jc], p['moe_w_gate'][jc], p['moe_w_up'][jc], p['moe_w_down'][jc])
        x = x + g2[:, None, :] * f
    y = _rms(x) * p['final_norm_w']
    return y, jnp.stack(ssm_out), jnp.stack(conv_out), jnp.stack(k_out), jnp.stack(v_out)


def setup_inputs(seed: int = 0) -> dict:
    key = jax.random.key(seed)
    ks = iter(list(jax.random.split(key, 32)))

    def nrm(shape, scale):
        return scale * jax.random.normal(next(ks), shape, jnp.float32)

    dt0 = jnp.exp(jax.random.uniform(next(ks), (N_SSD_LAYERS, SSD_HEADS), jnp.float32,
                                     math.log(DT_MIN), math.log(DT_MAX)))
    ssd_dt_bias = dt0 + jnp.log(-jnp.expm1(-dt0))
    ssd_a_log = jnp.log(jax.random.uniform(next(ks), (N_SSD_LAYERS, SSD_HEADS), jnp.float32, 1.0, 16.0))
    return {
        'x_prompt': nrm((BATCH, SEQ, D_MODEL), 1.0),
        'x_sample': nrm((DEC_BATCH, DEC_SEQ, D_MODEL), 1.0),
        'state_ssm': nrm((N_SSD_LAYERS, DEC_BATCH, SSD_HEADS, SSD_HEAD_DIM, SSD_STATE), 0.1),
        'state_conv': nrm((N_SSD_LAYERS, DEC_BATCH, SSD_CONV - 1, SSD_CONV_DIM), 1.0),
        'cache_k': nrm((N_SB_LAYERS, DEC_BATCH, PAST_LEN, SB_HEADS, SB_HEAD_DIM), 1.0),
        'cache_v': nrm((N_SB_LAYERS, DEC_BATCH, PAST_LEN, SB_HEADS, SB_HEAD_DIM), 1.0),
        'c_prompt': nrm((BATCH, D_MODEL), 1.0),
        'c_sample': nrm((DEC_BATCH, D_MODEL), 1.0),
        'w_mod': nrm((DEPTH, D_MODEL, 6 * D_MODEL), D_MODEL ** -0.5),
        'b_mod': nrm((DEPTH, 6 * D_MODEL), 0.01),
        'ssd_w_in': nrm((N_SSD_LAYERS, D_MODEL, SSD_IN_DIM), D_MODEL ** -0.5),
        'ssd_conv_w': nrm((N_SSD_LAYERS, SSD_CONV, SSD_CONV_DIM), SSD_CONV ** -0.5),
        'ssd_conv_b': nrm((N_SSD_LAYERS, SSD_CONV_DIM), 0.01),
        'ssd_dt_bias': ssd_dt_bias,
        'ssd_a_log': ssd_a_log,
        'ssd_d': 1.0 + nrm((N_SSD_LAYERS, SSD_HEADS), 0.1),
        'ssd_norm_w': 1.0 + nrm((N_SSD_LAYERS, SSD_INNER), 0.01),
        'ssd_w_out': nrm((N_SSD_LAYERS, SSD_INNER, D_MODEL), SSD_INNER ** -0.5),
        'sb_w_qkv': nrm((N_SB_LAYERS, D_MODEL, 3 * SB_HEADS * SB_HEAD_DIM), D_MODEL ** -0.5),
        'sb_w_o': nrm((N_SB_LAYERS, SB_HEADS * SB_HEAD_DIM, D_MODEL), (SB_HEADS * SB_HEAD_DIM) ** -0.5),
        'ffn_w_gate': nrm((N_DENSE_LAYERS, D_MODEL, D_FF), D_MODEL ** -0.5),
        'ffn_w_up': nrm((N_DENSE_LAYERS, D_MODEL, D_FF), D_MODEL ** -0.5),
        'ffn_w_down': nrm((N_DENSE_LAYERS, D_FF, D_MODEL), D_FF ** -0.5),
        'moe_w_router': nrm((N_MOE_LAYERS, D_MODEL, N_EXPERTS), D_MODEL ** -0.5),
        'moe_w_gate': nrm((N_MOE_LAYERS, N_EXPERTS, D_MODEL, D_FF), D_MODEL ** -0.5),
        'moe_w_up': nrm((N_MOE_LAYERS, N_EXPERTS, D_MODEL, D_FF), D_MODEL ** -0.5),
        'moe_w_down': nrm((N_MOE_LAYERS, N_EXPERTS, D_FF, D_MODEL), D_FF ** -0.5),
        'final_norm_w': 1.0 + nrm((D_MODEL,), 0.01),
    }


def reference(x_prompt, x_sample, state_ssm, state_conv, cache_k, cache_v, c_prompt, c_sample,
              w_mod, b_mod, ssd_w_in, ssd_conv_w, ssd_conv_b, ssd_dt_bias, ssd_a_log, ssd_d,
              ssd_norm_w, ssd_w_out, sb_w_qkv, sb_w_o, ffn_w_gate, ffn_w_up, ffn_w_down,
              moe_w_router, moe_w_gate, moe_w_up, moe_w_down, final_norm_w):
    p = dict(w_mod=w_mod, b_mod=b_mod, ssd_w_in=ssd_w_in, ssd_conv_w=ssd_conv_w, ssd_conv_b=ssd_conv_b,
             ssd_dt_bias=ssd_dt_bias, ssd_a_log=ssd_a_log, ssd_d=ssd_d, ssd_norm_w=ssd_norm_w,
             ssd_w_out=ssd_w_out, sb_w_qkv=sb_w_qkv, sb_w_o=sb_w_o, ffn_w_gate=ffn_w_gate,
             ffn_w_up=ffn_w_up, ffn_w_down=ffn_w_down, moe_w_router=moe_w_router,
             moe_w_gate=moe_w_gate, moe_w_up=moe_w_up, moe_w_down=moe_w_down,
             final_norm_w=final_norm_w)
    bp = x_prompt.shape[0]
    ssm0 = jnp.zeros((N_SSD_LAYERS, bp, SSD_HEADS, SSD_HEAD_DIM, SSD_STATE), x_prompt.dtype)
    conv0 = jnp.zeros((N_SSD_LAYERS, bp, SSD_CONV - 1, SSD_CONV_DIM), x_prompt.dtype)
    y_prompt, ssm_p, conv_p, k_p, v_p = _trunk(x_prompt, c_prompt, ssm0, conv0, None, None, p)
    y_sample, ssm_s, conv_s, k_s, v_s = _trunk(x_sample, c_sample, state_ssm, state_conv, cache_k, cache_v, p)
    return (y_prompt, y_sample, ssm_p, conv_p, k_p, v_p, ssm_s, conv_s, k_s, v_s)
```

```python
import functools
import math

import numpy as np
import jax
import jax.numpy as jnp
from jax import lax
from jax.experimental import pallas as pl
from jax.experimental.pallas import tpu as pltpu

F32 = jnp.float32
BF16 = jnp.bfloat16
EPS = 1e-6
SSD_CHUNK = 64
SB_BLOCK = 128
TOP_K = 2
LANES = 128
SUBLANES = 8
VMEM_LIMIT = 56 * 1024 * 1024


def _tile(n, pref, align=LANES):
    t = min(pref, n)
    t -= t % align
    while t >= align:
        if n % t == 0:
            return t
        t -= align
    return n


def _params(*sem):
    return pltpu.CompilerParams(dimension_semantics=sem, vmem_limit_bytes=VMEM_LIMIT)


def _sigmoid(x):
    return 1.0 / (1.0 + jnp.exp(-x))


def _silu(x):
    return x * _sigmoid(x)


def _softplus(x):
    return jnp.maximum(x, 0.0) + jnp.log1p(jnp.exp(-jnp.abs(x)))


def _split_bf16(x, parts):
    out = []
    r = x
    for _ in range(parts - 1):
        p = r.astype(BF16)
        out.append(p)
        r = r - p.astype(F32)
    out.append(r.astype(BF16))
    return out


def _dot(a, b):
    return jnp.dot(a, b, preferred_element_type=F32)


def _dot_nt(a, b):
    return lax.dot_general(a, b, (((1,), (1,)), ((), ())), preferred_element_type=F32)


def _sel_dot(x, onehot, parts=3):
    acc = None
    for p in _split_bf16(x, parts):
        t = _dot(p, onehot)
        acc = t if acc is None else acc + t
    return acc


def _sel_dot_left(onehot, x, parts=3):
    acc = None
    for p in _split_bf16(x, parts):
        t = _dot(onehot, p)
        acc = t if acc is None else acc + t
    return acc


def _mod_kernel(c_ref, w_ref, b_ref, o_ref):
    cs = _silu(c_ref[...]).astype(BF16)
    o_ref[0] = _dot(cs, w_ref[0].astype(BF16)) + b_ref[0]


def _mod_call(c_pad, w_mod, b_mod):
    depth, d, n = w_mod.shape
    rows = c_pad.shape[0]
    tn = _tile(n, 512)
    return pl.pallas_call(
        _mod_kernel,
        grid=(depth, n // tn),
        in_specs=[pl.BlockSpec((rows, d), lambda i, j: (0, 0)),
                  pl.BlockSpec((1, d, tn), lambda i, j: (i, 0, j)),
                  pl.BlockSpec((1, 1, tn), lambda i, j: (i, 0, j))],
        out_specs=pl.BlockSpec((1, rows, tn), lambda i, j: (i, 0, j)),
        out_shape=jax.ShapeDtypeStruct((depth, rows, n), F32),
        compiler_params=_params("arbitrary", "arbitrary"),
        name="adaln_mod",
    )(c_pad, w_mod, b_mod.reshape(depth, 1, n))


def _modulate_kernel(x_ref, sh_ref, sc_ref, o_ref):
    x = x_ref[...]
    ms = jnp.mean(x * x, axis=-1, keepdims=True)
    h = x * lax.rsqrt(ms + EPS) * (1.0 + sc_ref[...]) + sh_ref[...]
    o_ref[...] = h.astype(o_ref.dtype)


def _modulate_call(x3, sh, sc, out_dtype):
    g, tg, d = x3.shape
    gb = _tile(g, max(1, 256 // tg), 1)
    return pl.pallas_call(
        _modulate_kernel,
        grid=(g // gb,),
        in_specs=[pl.BlockSpec((gb, tg, d), lambda i: (i, 0, 0)),
                  pl.BlockSpec((gb, 1, d), lambda i: (i, 0, 0)),
                  pl.BlockSpec((gb, 1, d), lambda i: (i, 0, 0))],
        out_specs=pl.BlockSpec((gb, tg, d), lambda i: (i, 0, 0)),
        out_shape=jax.ShapeDtypeStruct((g, tg, d), out_dtype),
        compiler_params=_params("arbitrary"),
        name="adaln_modulate",
    )(x3, sh, sc)


def _modulate_router_kernel(x_ref, sh_ref, sc_ref, wr_ref, o_ref, r_ref, *, n_experts):
    x = x_ref[...]
    gb, tg, d = x.shape
    ms = jnp.mean(x * x, axis=-1, keepdims=True)
    h = x * lax.rsqrt(ms + EPS) * (1.0 + sc_ref[...]) + sh_ref[...]
    o_ref[...] = h
    h2 = h.reshape(gb * tg, d)
    h_hi, h_lo = _split_bf16(h2, 2)
    w_hi, w_lo = _split_bf16(wr_ref[...], 2)
    logits = _dot(h_hi, w_hi) + _dot(h_hi, w_lo) + _dot(h_lo, w_hi)
    lane = lax.broadcasted_iota(jnp.int32, logits.shape, 1).astype(F32)
    neg = jnp.float32(-jnp.inf)
    lg = jnp.where(lane < n_experts, logits, neg)
    m1 = jnp.max(lg, axis=1, keepdims=True)
    i1 = jnp.min(jnp.where(lg == m1, lane, float(LANES)), axis=1, keepdims=True)
    lg2 = jnp.where(lane == i1, neg, lg)
    m2 = jnp.max(lg2, axis=1, keepdims=True)
    i2 = jnp.min(jnp.where(lg2 == m2, lane, float(LANES)), axis=1, keepdims=True)
    e2 = jnp.exp(m2 - m1)
    g1 = 1.0 / (1.0 + e2)
    g2 = e2 / (1.0 + e2)
    out = jnp.where(lane == 0.0, i1,
                    jnp.where(lane == 1.0, i2,
                              jnp.where(lane == 2.0, g1, jnp.where(lane == 3.0, g2, 0.0))))
    r_ref[...] = out.reshape(gb, tg, LANES)


def _modulate_router_call(x3, sh, sc, w_router):
    g, tg, d = x3.shape
    n_experts = w_router.shape[1]
    wr = jnp.zeros((d, LANES), F32).at[:, :n_experts].set(w_router)
    gb = _tile(g, max(1, 256 // tg), 1)
    return pl.pallas_call(
        functools.partial(_modulate_router_kernel, n_experts=n_experts),
        grid=(g // gb,),
        in_specs=[pl.BlockSpec((gb, tg, d), lambda i: (i, 0, 0)),
                  pl.BlockSpec((gb, 1, d), lambda i: (i, 0, 0)),
                  pl.BlockSpec((gb, 1, d), lambda i: (i, 0, 0)),
                  pl.BlockSpec((d, LANES), lambda i: (0, 0))],
        out_specs=[pl.BlockSpec((gb, tg, d), lambda i: (i, 0, 0)),
                   pl.BlockSpec((gb, tg, LANES), lambda i: (i, 0, 0))],
        out_shape=[jax.ShapeDtypeStruct((g, tg, d), F32),
                   jax.ShapeDtypeStruct((g, tg, LANES), F32)],
        compiler_params=_params("arbitrary"),
        name="adaln_modulate_router",
    )(x3, sh, sc, wr)


def _ws_kernel(a_ref, w_ref, o_ref, wb_ref):
    @pl.when(pl.program_id(1) == 0)
    def _():
        wb_ref[...] = w_ref[...].astype(BF16)

    o_ref[...] = _dot(a_ref[...], wb_ref[...])


def _matmul_ws(a, w, col_off, n_cols, tn, tm, name):
    m, k = a.shape
    off = col_off // tn
    assert col_off % tn == 0 and n_cols % tn == 0 and m % tm == 0
    return pl.pallas_call(
        _ws_kernel,
        grid=(n_cols // tn, m // tm),
        in_specs=[pl.BlockSpec((tm, k), lambda n, i: (i, 0)),
                  pl.BlockSpec((k, tn), lambda n, i: (0, n + off))],
        out_specs=pl.BlockSpec((tm, tn), lambda n, i: (i, n)),
        out_shape=jax.ShapeDtypeStruct((m, n_cols), F32),
        scratch_shapes=[pltpu.VMEM((k, tn), BF16)],
        compiler_params=_params("arbitrary", "arbitrary"),
        name=name,
    )(a, w)


def _swiglu_kernel(te_ref, tv_ref, a_ref, wg_ref, wu_ref, o_ref, wgb_ref, wub_ref):
    i = pl.program_id(1)
    fresh = jnp.logical_or(i == 0, te_ref[i] != te_ref[jnp.maximum(i - 1, 0)])

    @pl.when(fresh)
    def _():
        wgb_ref[...] = wg_ref[0].astype(BF16)
        wub_ref[...] = wu_ref[0].astype(BF16)

    @pl.when(tv_ref[i] != 0)
    def _():
        a = a_ref[...]
        g = _dot(a, wgb_ref[...])
        u = _dot(a, wub_ref[...])
        o_ref[...] = (_silu(g) * u).astype(o_ref.dtype)

    @pl.when(tv_ref[i] == 0)
    def _():
        o_ref[...] = jnp.zeros_like(o_ref)


def _swiglu_call(a, w_gate, w_up, tile_expert, tile_valid, tm, tf, name):
    m, k = a.shape
    _, _, f = w_gate.shape
    nt = m // tm
    grid_spec = pltpu.PrefetchScalarGridSpec(
        num_scalar_prefetch=2,
        grid=(f // tf, nt),
        in_specs=[pl.BlockSpec((tm, k), lambda j, i, te, tv: (i, 0)),
                  pl.BlockSpec((1, k, tf), lambda j, i, te, tv: (te[i], 0, j)),
                  pl.BlockSpec((1, k, tf), lambda j, i, te, tv: (te[i], 0, j))],
        out_specs=pl.BlockSpec((tm, tf), lambda j, i, te, tv: (i, j)),
        scratch_shapes=[pltpu.VMEM((k, tf), BF16), pltpu.VMEM((k, tf), BF16)],
    )
    return pl.pallas_call(
        _swiglu_kernel,
        grid_spec=grid_spec,
        out_shape=jax.ShapeDtypeStruct((m, f), BF16),
        compiler_params=_params("arbitrary", "arbitrary"),
        name=name,
    )(tile_expert, tile_valid, a, w_gate, w_up)


def _kt_accumulate(tv_ref, a_ref, w_ref, acc_ref):
    k = pl.program_id(2)

    @pl.when(k == 0)
    def _():
        acc_ref[...] = jnp.zeros_like(acc_ref)

    @pl.when(tv_ref[pl.program_id(0)] != 0)
    def _():
        acc_ref[...] += _dot(a_ref[...], w_ref[0].astype(BF16))


def _kt_residual_kernel(te_ref, tv_ref, a_ref, w_ref, x_ref, g_ref, o_ref, acc_ref):
    _kt_accumulate(tv_ref, a_ref, w_ref, acc_ref)

    @pl.when(pl.program_id(2) == pl.num_programs(2) - 1)
    def _():
        o_ref[...] = x_ref[...] + g_ref[...] * acc_ref[...].reshape(o_ref.shape)


def _kt_rowscale_kernel(te_ref, tv_ref, a_ref, w_ref, rs_ref, o_ref, acc_ref):
    _kt_accumulate(tv_ref, a_ref, w_ref, acc_ref)

    @pl.when(pl.program_id(2) == pl.num_programs(2) - 1)
    def _():
        o_ref[...] = rs_ref[0] * acc_ref[...]


def _matmul_residual(a, w, x3, gate3, tm, tn, tk, name):
    m, k = a.shape
    n = w.shape[-1]
    g, tg, _ = x3.shape
    gb = tm // tg
    nt = m // tm
    te = jnp.zeros((nt,), jnp.int32)
    tv = jnp.ones((nt,), jnp.int32)
    grid_spec = pltpu.PrefetchScalarGridSpec(
        num_scalar_prefetch=2,
        grid=(nt, n // tn, k // tk),
        in_specs=[pl.BlockSpec((tm, tk), lambda i, j, kk, te, tv: (i, kk)),
                  pl.BlockSpec((1, tk, tn), lambda i, j, kk, te, tv: (te[i], kk, j)),
                  pl.BlockSpec((gb, tg, tn), lambda i, j, kk, te, tv: (i, 0, j)),
                  pl.BlockSpec((gb, 1, tn), lambda i, j, kk, te, tv: (i, 0, j))],
        out_specs=pl.BlockSpec((gb, tg, tn), lambda i, j, kk, te, tv: (i, 0, j)),
        scratch_shapes=[pltpu.VMEM((tm, tn), F32)],
    )
    return pl.pallas_call(
        _kt_residual_kernel,
        grid_spec=grid_spec,
        out_shape=jax.ShapeDtypeStruct(x3.shape, F32),
        compiler_params=_params("arbitrary", "arbitrary", "arbitrary"),
        name=name,
    )(te, tv, a, w.reshape((1,) + w.shape[-2:]), x3, gate3)


def _matmul_rowscale(a, w, rowscale, tile_expert, tile_valid, tm, tn, tk, name):
    m, k = a.shape
    n = w.shape[-1]
    nt = m // tm
    grid_spec = pltpu.PrefetchScalarGridSpec(
        num_scalar_prefetch=2,
        grid=(nt, n // tn, k // tk),
        in_specs=[pl.BlockSpec((tm, tk), lambda i, j, kk, te, tv: (i, kk)),
                  pl.BlockSpec((1, tk, tn), lambda i, j, kk, te, tv: (te[i], kk, j)),
                  pl.BlockSpec((1, tm, 1), lambda i, j, kk, te, tv: (i, 0, 0))],
        out_specs=pl.BlockSpec((tm, tn), lambda i, j, kk, te, tv: (i, j)),
        scratch_shapes=[pltpu.VMEM((tm, tn), F32)],
    )
    return pl.pallas_call(
        _kt_rowscale_kernel,
        grid_spec=grid_spec,
        out_shape=jax.ShapeDtypeStruct((m, n), F32),
        compiler_params=_params("arbitrary", "arbitrary", "arbitrary"),
        name=name,
    )(tile_expert, tile_valid, a, w, rowscale.reshape(nt, tm, 1))


def _conv_kernel(x_ref, prev_ref, w_ref, b_ref, o_ref, xp_ref, *, tl, taps):
    t = pl.program_id(2)
    hist = taps - 1
    top = SUBLANES - hist

    @pl.when(t == 0)
    def _():
        xp_ref[top:SUBLANES, :] = prev_ref[0]

    @pl.when(t > 0)
    def _():
        xp_ref[0:SUBLANES, :] = xp_ref[tl:tl + SUBLANES, :]

    x = x_ref[...]
    xp_ref[SUBLANES:SUBLANES + tl, :] = x
    w = w_ref[...]
    acc = b_ref[...] + w[hist:taps] * x
    for k in range(hist):
        acc = acc + w[k:k + 1] * xp_ref[top + k:top + k + tl, :]
    o_ref[...] = _silu(acc)


def _conv_call(proj, row0, nb, seq, col0, conv_prev, conv_w, conv_b, name):
    taps, c = conv_w.shape
    tl = _tile(seq, 512, SUBLANES)
    tc = _tile(math.gcd(c, col0) if col0 else c, 1024)
    nt = seq // tl
    assert row0 % tl == 0 and col0 % tc == 0
    rb0, cb0 = row0 // tl, col0 // tc
    return pl.pallas_call(
        functools.partial(_conv_kernel, tl=tl, taps=taps),
        grid=(nb, c // tc, nt),
        in_specs=[pl.BlockSpec((tl, tc), lambda b, j, t: (rb0 + b * nt + t, cb0 + j)),
                  pl.BlockSpec((1, taps - 1, tc), lambda b, j, t: (b, 0, j)),
                  pl.BlockSpec((taps, tc), lambda b, j, t: (0, j)),
                  pl.BlockSpec((1, tc), lambda b, j, t: (0, j))],
        out_specs=pl.BlockSpec((tl, tc), lambda b, j, t: (b * nt + t, j)),
        out_shape=jax.ShapeDtypeStruct((nb * seq, c), F32),
        scratch_shapes=[pltpu.VMEM((tl + SUBLANES, tc), F32)],
        compiler_params=_params("arbitrary", "arbitrary", "arbitrary"),
        name=name,
    )(proj, conv_prev, conv_w, conv_b.reshape(1, c))


def _ssd_kernel(x_ref, b_ref, c_ref, dtr_ref, z_ref, sel_ref, e_ref, bias_ref, alog_ref, dskip_ref, nw_ref,
                s0_ref, g_ref, s_ref, st_ref, *, q, heads, hdim):
    c = pl.program_id(2)
    rp = heads * hdim
    per_lane = LANES // hdim

    @pl.when(c == 0)
    def _():
        st_ref[...] = s0_ref[0, 0].T

    dt_all = _softplus(dtr_ref[...] + bias_ref[...])
    a_all = -jnp.exp(alog_ref[...])
    sel = sel_ref[0]
    dtg = _sel_dot(dt_all, sel)
    dag = _sel_dot(dt_all * a_all, sel)
    ri = lax.broadcasted_iota(jnp.int32, (q, q), 0)
    ci = lax.broadcasted_iota(jnp.int32, (q, q), 1)
    causal = ri >= ci
    tril = jnp.where(causal, 1.0, 0.0).astype(BF16)
    acs = _sel_dot_left(tril, dag)
    acs_t = acs.T
    expand = e_ref[...]
    dt_exp = _sel_dot(dtg, expand)
    acs_exp = _sel_dot(acs, expand)

    x = x_ref[...]
    xdt = x * dt_exp
    bm = b_ref[...].astype(BF16)
    cm = c_ref[...].astype(BF16)
    cb = _dot_nt(cm, bm)

    st = st_ref[...]
    y = _dot(cm, st.astype(BF16)) * jnp.exp(acs_exp)
    last = acs_exp[q - 1:q, :]
    xs = (xdt * jnp.exp(last - acs_exp)).astype(BF16)
    bt = b_ref[...].T.astype(BF16)
    st_ref[...] = st * jnp.exp(last) + _dot(bt, xs)

    lane = lax.broadcasted_iota(jnp.int32, (q, LANES), 1)
    xdt_b = xdt
    pieces = []
    for pr in range(heads // per_lane):
        xp = xdt_b[:, pr * LANES:(pr + 1) * LANES]
        acc = None
        for hh in range(per_lane):
            r = pr * per_lane + hh
            seg = acs[:, r:r + 1] - acs_t[r:r + 1, :]
            dec = jnp.exp(jnp.where(causal, seg, -jnp.inf))
            m = (cb * dec).astype(BF16)
            in_head = jnp.logical_and(lane >= hh * hdim, lane < (hh + 1) * hdim)
            xm = jnp.where(in_head, xp, 0.0).astype(BF16)
            t = _dot(m, xm)
            acc = t if acc is None else acc + t
        pieces.append(acc)
    y = y + jnp.concatenate(pieces, axis=1) + dskip_ref[...] * x

    gt = y * _silu(z_ref[...])
    ms = jnp.mean(gt * gt, axis=-1, keepdims=True)
    g_ref[...] = (gt * lax.rsqrt(ms + EPS) * nw_ref[...]).astype(g_ref.dtype)

    @pl.when(c == pl.num_programs(2) - 1)
    def _():
        s_ref[0, 0] = st_ref[...].T


def _ssd_call(xbc, proj, dt_raw, row0, nb, seq, s0, consts, name):
    sel, expand, bias, alog, dskip, nw = consts
    groups, n_heads, _ = sel.shape
    _, _, rp, n = s0.shape
    heads = n_heads // groups
    hdim = rp // heads
    inner = groups * rp
    q = SSD_CHUNK if seq % SSD_CHUNK == 0 else seq
    nc = seq // q
    assert row0 % q == 0 and LANES % hdim == 0 and heads % (LANES // hdim) == 0 and heads <= LANES
    rb0 = row0 // q
    nb_cols = inner // n
    return pl.pallas_call(
        functools.partial(_ssd_kernel, q=q, heads=heads, hdim=hdim),
        grid=(nb, groups, nc),
        in_specs=[pl.BlockSpec((q, rp), lambda b, g, c: (b * nc + c, g)),
                  pl.BlockSpec((q, n), lambda b, g, c: (b * nc + c, nb_cols + g)),
                  pl.BlockSpec((q, n), lambda b, g, c: (b * nc + c, nb_cols + groups + g)),
                  pl.BlockSpec((q, n_heads), lambda b, g, c: (rb0 + b * nc + c, 0)),
                  pl.BlockSpec((q, rp), lambda b, g, c: (rb0 + b * nc + c, g)),
                  pl.BlockSpec((1, n_heads, LANES), lambda b, g, c: (g, 0, 0)),
                  pl.BlockSpec((LANES, rp), lambda b, g, c: (0, 0)),
                  pl.BlockSpec((1, n_heads), lambda b, g, c: (0, 0)),
                  pl.BlockSpec((1, n_heads), lambda b, g, c: (0, 0)),
                  pl.BlockSpec((1, rp), lambda b, g, c: (0, g)),
                  pl.BlockSpec((1, rp), lambda b, g, c: (0, g)),
                  pl.BlockSpec((1, 1, rp, n), lambda b, g, c: (b, g, 0, 0))],
        out_specs=[pl.BlockSpec((q, rp), lambda b, g, c: (b * nc + c, g)),
                   pl.BlockSpec((1, 1, rp, n), lambda b, g, c: (b, g, 0, 0))],
        out_shape=[jax.ShapeDtypeStruct((nb * seq, inner), BF16),
                   jax.ShapeDtypeStruct(s0.shape, F32)],
        scratch_shapes=[pltpu.VMEM((n, rp), F32)],
        compiler_params=_params("arbitrary", "arbitrary", "arbitrary"),
        name=name,
    )(xbc, xbc, xbc, dt_raw, proj, sel, expand, bias, alog, dskip, nw, s0)


def _sb_block(q, k, v, carry, acc, suffix, scale, strict_mask):
    z = _dot_nt(q, k) * scale
    tail = jnp.log1p(jnp.exp(-jnp.abs(z)))
    log_beta = jnp.minimum(z, 0.0) - tail
    log_keep = jnp.minimum(-z, 0.0) - tail
    if strict_mask is not None:
        log_keep = jnp.where(strict_mask, log_keep, 0.0)
    later = _sel_dot(log_keep, suffix, parts=2) + carry
    w = jnp.exp(log_beta + later)
    if strict_mask is not None:
        w = jnp.where(strict_mask, w, 0.0)
    acc = acc + _dot(w.astype(BF16), v)
    carry = carry + jnp.sum(log_keep, axis=1, keepdims=True)
    return carry, acc


def _suffix_matrix(n):
    ri = lax.broadcasted_iota(jnp.int32, (n, n), 0)
    ci = lax.broadcasted_iota(jnp.int32, (n, n), 1)
    return jnp.where(ri > ci, 1.0, 0.0).astype(BF16), ci < ri


def _sb_prompt_kernel(q_ref, k_ref, v_ref, o_ref, *, blk, scale):
    qi = pl.program_id(2)
    q = q_ref[...].astype(BF16)
    d = q.shape[1]
    suffix, strict = _suffix_matrix(blk)

    def load(kb):
        ks = pl.multiple_of(kb * blk, blk)
        return k_ref[pl.ds(ks, blk), :].astype(BF16), v_ref[pl.ds(ks, blk), :].astype(BF16)

    k, v = load(qi)
    carry, acc = _sb_block(q, k, v, jnp.zeros((blk, 1), F32), jnp.zeros((blk, d), F32), suffix, scale, strict)

    def body(i, ca):
        k, v = load(qi - 1 - i)
        return _sb_block(q, k, v, ca[0], ca[1], suffix, scale, None)

    carry, acc = lax.fori_loop(0, qi, body, (carry, acc))
    o_ref[...] = acc.astype(o_ref.dtype)


def _sb_prompt_call(qkv, row0, nb, seq, n_heads, hd, scale, name):
    blk = SB_BLOCK
    nq = seq // blk
    assert row0 % seq == 0 and hd % LANES == 0
    sb0 = row0 // seq
    rb0 = row0 // blk
    return pl.pallas_call(
        functools.partial(_sb_prompt_kernel, blk=blk, scale=scale),
        grid=(nb, n_heads, nq),
        in_specs=[pl.BlockSpec((blk, hd), lambda b, h, i: (rb0 + b * nq + i, h)),
                  pl.BlockSpec((seq, hd), lambda b, h, i: (sb0 + b, n_heads + h)),
                  pl.BlockSpec((seq, hd), lambda b, h, i: (sb0 + b, 2 * n_heads + h))],
        out_specs=pl.BlockSpec((blk, hd), lambda b, h, i: (b * nq + i, h)),
        out_shape=jax.ShapeDtypeStruct((nb * seq, n_heads * hd), BF16),
        compiler_params=_params("arbitrary", "arbitrary", "arbitrary"),
        name=name,
    )(qkv, qkv, qkv)


def _sb_decode_kernel(q_ref, kn_ref, vn_ref, kc_ref, vc_ref, o_ref, *, kblk, scale):
    q = q_ref[...].astype(BF16)
    t, d = q.shape
    past = kc_ref.shape[1]
    suffix_new, strict = _suffix_matrix(t)
    carry, acc = _sb_block(q, kn_ref[...].astype(BF16), vn_ref[...].astype(BF16),
                           jnp.zeros((t, 1), F32), jnp.zeros((t, d), F32), suffix_new, scale, strict)
    suffix, _ = _suffix_matrix(kblk)
    nkb = past // kblk

    def body(i, ca):
        ks = pl.multiple_of((nkb - 1 - i) * kblk, kblk)
        k = kc_ref[0, pl.ds(ks, kblk), :].astype(BF16)
        v = vc_ref[0, pl.ds(ks, kblk), :].astype(BF16)
        return _sb_block(q, k, v, ca[0], ca[1], suffix, scale, None)

    carry, acc = lax.fori_loop(0, nkb, body, (carry, acc))
    o_ref[...] = acc.astype(o_ref.dtype)


def _sb_decode_call(qkv, row0, nb, seq, n_heads, hd, cache_k, cache_v, scale, name):
    past = cache_k.shape[1]
    kblk = _tile(past, 256)
    assert row0 % seq == 0
    sb0 = row0 // seq
    kc = cache_k.reshape(nb, past, n_heads * hd)
    vc = cache_v.reshape(nb, past, n_heads * hd)
    return pl.pallas_call(
        functools.partial(_sb_decode_kernel, kblk=kblk, scale=scale),
        grid=(nb, n_heads),
        in_specs=[pl.BlockSpec((seq, hd), lambda b, h: (sb0 + b, h)),
                  pl.BlockSpec((seq, hd), lambda b, h: (sb0 + b, n_heads + h)),
                  pl.BlockSpec((seq, hd), lambda b, h: (sb0 + b, 2 * n_heads + h)),
                  pl.BlockSpec((1, past, hd), lambda b, h: (b, 0, h)),
                  pl.BlockSpec((1, past, hd), lambda b, h: (b, 0, h))],
        out_specs=pl.BlockSpec((seq, hd), lambda b, h: (b, h)),
        out_shape=jax.ShapeDtypeStruct((nb * seq, n_heads * hd), BF16),
        compiler_params=_params("arbitrary", "arbitrary"),
        name=name,
    )(qkv, qkv, qkv, kc, vc)


def _gather_cast_kernel(src_ref, h_ref, o_ref, buf_ref, sem):
    tm = buf_ref.shape[0]

    def row_copy(r):
        return pltpu.make_async_copy(h_ref.at[pl.ds(src_ref[0, 0, r], 1), :], buf_ref.at[pl.ds(r, 1), :], sem)

    def start(r, carry):
        row_copy(r).start()
        return carry

    def wait(r, carry):
        row_copy(r).wait()
        return carry

    lax.fori_loop(0, tm, start, 0)
    lax.fori_loop(0, tm, wait, 0)
    o_ref[...] = buf_ref[...].astype(o_ref.dtype)


def _gather_cast_call(h, src_rows, tm):
    m, d = h.shape
    nt = src_rows.shape[0] // tm
    return pl.pallas_call(
        _gather_cast_kernel,
        grid=(nt,),
        in_specs=[pl.BlockSpec((1, 1, tm), lambda i: (i, 0, 0), memory_space=pltpu.SMEM),
                  pl.BlockSpec(memory_space=pl.ANY)],
        out_specs=pl.BlockSpec((tm, d), lambda i: (i, 0)),
        out_shape=jax.ShapeDtypeStruct((nt * tm, d), BF16),
        scratch_shapes=[pltpu.VMEM((tm, d), F32), pltpu.SemaphoreType.DMA(())],
        compiler_params=_params("arbitrary"),
        name="moe_gather_rows",
    )(src_rows.reshape(nt, 1, tm), h)


def _combine_kernel(p1_ref, p2_ref, ys_ref, x_ref, g_ref, o_ref, buf_ref, sem):
    tm = buf_ref.shape[1]

    def copies(r):
        return (pltpu.make_async_copy(ys_ref.at[pl.ds(p1_ref[0, 0, r], 1), :], buf_ref.at[0, pl.ds(r, 1), :], sem),
                pltpu.make_async_copy(ys_ref.at[pl.ds(p2_ref[0, 0, r], 1), :], buf_ref.at[1, pl.ds(r, 1), :], sem))

    def start(r, carry):
        for cp in copies(r):
            cp.start()
        return carry

    def wait(r, carry):
        for cp in copies(r):
            cp.wait()
        return carry

    lax.fori_loop(0, tm, start, 0)
    lax.fori_loop(0, tm, wait, 0)
    f = (buf_ref[0] + buf_ref[1]).reshape(o_ref.shape)
    o_ref[...] = x_ref[...] + g_ref[...] * f


def _combine_call(ys, pos1, pos2, x3, gate3, tm):
    g, tg, d = x3.shape
    gb = tm // tg
    nt = (g * tg) // tm
    return pl.pallas_call(
        _combine_kernel,
        grid=(nt,),
        in_specs=[pl.BlockSpec((1, 1, tm), lambda i: (i, 0, 0), memory_space=pltpu.SMEM),
                  pl.BlockSpec((1, 1, tm), lambda i: (i, 0, 0), memory_space=pltpu.SMEM),
                  pl.BlockSpec(memory_space=pl.ANY),
                  pl.BlockSpec((gb, tg, d), lambda i: (i, 0, 0)),
                  pl.BlockSpec((gb, 1, d), lambda i: (i, 0, 0))],
        out_specs=pl.BlockSpec((gb, tg, d), lambda i: (i, 0, 0)),
        out_shape=jax.ShapeDtypeStruct(x3.shape, F32),
        scratch_shapes=[pltpu.VMEM((2, tm, d), F32), pltpu.SemaphoreType.DMA(())],
        compiler_params=_params("arbitrary"),
        name="moe_combine_rows",
    )(pos1.reshape(nt, 1, tm), pos2.reshape(nt, 1, tm), ys, x3, gate3)


def _final_norm_kernel(x_ref, w_ref, o_ref):
    x = x_ref[...]
    ms = jnp.mean(x * x, axis=-1, keepdims=True)
    o_ref[...] = x * lax.rsqrt(ms + EPS) * w_ref[...]


def _final_norm_call(x2, w):
    m, d = x2.shape
    tm = _tile(m, 256, SUBLANES)
    return pl.pallas_call(
        _final_norm_kernel,
        grid=(m // tm,),
        in_specs=[pl.BlockSpec((tm, d), lambda i: (i, 0)), pl.BlockSpec((1, d), lambda i: (0, 0))],
        out_specs=pl.BlockSpec((tm, d), lambda i: (i, 0)),
        out_shape=jax.ShapeDtypeStruct((m, d), F32),
        compiler_params=_params("arbitrary"),
        name="final_norm",
    )(x2, w.reshape(1, d))


def _ssd_constants(groups, n_heads, hdim, dt_bias, a_log, d_skip, norm_w):
    heads = n_heads // groups
    sel = np.zeros((groups, n_heads, LANES), np.float32)
    for g in range(groups):
        for r in range(heads):
            sel[g, g * heads + r, r] = 1.0
    expand = np.zeros((LANES, heads * hdim), np.float32)
    for r in range(heads):
        expand[r, r * hdim:(r + 1) * hdim] = 1.0
    return (jnp.asarray(sel, BF16), jnp.asarray(expand, BF16),
            dt_bias.astype(F32).reshape(1, n_heads), a_log.astype(F32).reshape(1, n_heads),
            jnp.repeat(d_skip.astype(F32), hdim).reshape(1, n_heads * hdim), norm_w.reshape(1, n_heads * hdim))


def _route(route, n_experts, tm):
    m = route.shape[0]
    idx = route[:, :TOP_K].astype(jnp.int32)
    gates = route[:, TOP_K:2 * TOP_K]
    flat_e = idx.reshape(-1)
    n_assign = flat_e.shape[0]
    n_tiles = n_assign // tm + n_experts
    order = jnp.argsort(flat_e, stable=True)
    counts = jnp.zeros((n_experts,), jnp.int32).at[flat_e].add(1)
    padded = ((counts + tm - 1) // tm) * tm
    pad_end = jnp.cumsum(padded)
    pad_start = pad_end - padded
    cnt_start = jnp.cumsum(counts) - counts
    sorted_e = flat_e[order]
    dest_sorted = pad_start[sorted_e] + (jnp.arange(n_assign, dtype=jnp.int32) - cnt_start[sorted_e])
    dest = jnp.zeros((n_assign,), jnp.int32).at[order].set(dest_sorted)
    src_rows = jnp.zeros((n_tiles * tm,), jnp.int32).at[dest].set(jnp.arange(n_assign, dtype=jnp.int32) // TOP_K)
    row_gate = jnp.zeros((n_tiles * tm,), F32).at[dest].set(gates.reshape(-1))
    tile_start = jnp.arange(n_tiles, dtype=jnp.int32) * tm
    tile_expert = jnp.minimum(jnp.searchsorted(pad_end, tile_start, side="right"), n_experts - 1).astype(jnp.int32)
    tile_valid = (tile_start < pad_end[-1]).astype(jnp.int32)
    dest2 = dest.reshape(m, TOP_K)
    return src_rows, row_gate, tile_expert, tile_valid, dest2[:, 0], dest2[:, 1]


def kernel(x_prompt, x_sample, state_ssm, state_conv, cache_k, cache_v, c_prompt, c_sample, w_mod, b_mod, ssd_w_in, ssd_conv_w, ssd_conv_b, ssd_dt_bias, ssd_a_log, ssd_d, ssd_norm_w, ssd_w_out, sb_w_qkv, sb_w_o, ffn_w_gate, ffn_w_up, ffn_w_down, moe_w_router, moe_w_gate, moe_w_up, moe_w_down, final_norm_w):
    bp, lp, d = x_prompt.shape
    bs, ls, _ = x_sample.shape
    depth = w_mod.shape[0]
    mp, msamp = bp * lp, bs * ls
    m = mp + msamp
    tg = math.gcd(lp, ls)
    assert tg % SUBLANES == 0
    n_groups = m // tg
    n_heads = ssd_a_log.shape[1]
    hdim, n_state = state_ssm.shape[3], state_ssm.shape[4]
    inner = n_heads * hdim
    conv_dim = ssd_conv_w.shape[2]
    ssd_groups = (conv_dim - inner) // (2 * n_state)
    sb_heads, sb_hd = cache_k.shape[3], cache_k.shape[4]
    n_experts = moe_w_router.shape[2]
    d_ff = ffn_w_gate.shape[2]
    tm = _tile(m, 1024, tg)

    seq_of_group = np.concatenate([np.repeat(np.arange(bp), lp // tg), bp + np.repeat(np.arange(bs), ls // tg)])
    n_seq = bp + bs
    rows = -(-n_seq // 16) * 16
    c_all = jnp.zeros((rows, d), F32).at[:n_seq].set(jnp.concatenate([c_prompt, c_sample], axis=0))
    mod = _mod_call(c_all, w_mod, b_mod)
    mod_g = mod[:, seq_of_group, :].reshape(depth, n_groups, 1, 6, d)

    def mvec(i, which):
        return mod_g[i, :, :, which, :]

    x3 = jnp.concatenate([x_prompt.reshape(mp, d), x_sample.reshape(msamp, d)], axis=0).reshape(n_groups, tg, d)
    zeros_ssm = jnp.zeros((bp,) + state_ssm.shape[2:], F32)
    zeros_conv = jnp.zeros((bp,) + state_conv.shape[2:], F32)
    ssm_p, conv_p, k_p, v_p, ssm_s, conv_s, k_s, v_s = [], [], [], [], [], [], [], []

    for i in range(depth):
        j = i // 2
        h = _modulate_call(x3, mvec(i, 0), mvec(i, 1), BF16).reshape(m, d)
        if i % 2 == 0:
            w_in = ssd_w_in[j]
            n_main = inner + conv_dim
            proj = _matmul_ws(h, w_in, 0, n_main, _tile(n_main, 512), tm, "ssd_in_proj")
            dt_raw = _matmul_ws(h, w_in, n_main, n_heads, n_heads, tm, "ssd_in_proj_dt")
            consts = _ssd_constants(ssd_groups, n_heads, hdim, ssd_dt_bias[j], ssd_a_log[j], ssd_d[j], ssd_norm_w[j])
            outs = []
            for (row0, nb, seq, prev, s0, conv_out, ssm_out, tag) in (
                    (0, bp, lp, zeros_conv, zeros_ssm, conv_p, ssm_p, "prompt"),
                    (mp, bs, ls, state_conv[j], state_ssm[j], conv_s, ssm_s, "sample")):
                xbc = _conv_call(proj, row0, nb, seq, inner, prev, ssd_conv_w[j], ssd_conv_b[j], "ssd_conv_" + tag)
                s0g = s0.astype(F32).reshape(nb, ssd_groups, (n_heads // ssd_groups) * hdim, n_state)
                g_out, s_fin = _ssd_call(xbc, proj, dt_raw, row0, nb, seq, s0g, consts, "ssd_scan_" + tag)
                outs.append(g_out)
                ssm_out.append(s_fin.reshape(nb, n_heads, hdim, n_state))
                taps = ssd_conv_w.shape[1]
                raw = proj[row0:row0 + nb * seq].reshape(nb, seq, n_main)[:, seq - (taps - 1):, inner:]
                conv_out.append(raw)
            mix = jnp.concatenate(outs, axis=0)
            w_mix = ssd_w_out[j]
        else:
            qkv = _matmul_ws(h, sb_w_qkv[j], 0, 3 * sb_heads * sb_hd, _tile(3 * sb_heads * sb_hd, 512), tm, "sb_qkv")
            scale = float(sb_hd) ** -0.5
            o_p = _sb_prompt_call(qkv, 0, bp, lp, sb_heads, sb_hd, scale, "sb_attn_prompt")
            o_s = _sb_decode_call(qkv, mp, bs, ls, sb_heads, sb_hd, cache_k[j], cache_v[j], scale, "sb_attn_sample")
            mix = jnp.concatenate([o_p, o_s], axis=0)
            w_mix = sb_w_o[j]
            hd_all = sb_heads * sb_hd
            for (row0, nb, seq, k_out, v_out) in ((0, bp, lp, k_p, v_p), (mp, bs, ls, k_s, v_s)):
                blk = qkv[row0:row0 + nb * seq]
                k_out.append(blk[:, hd_all:2 * hd_all].reshape(nb, seq, sb_heads, sb_hd))
                v_out.append(blk[:, 2 * hd_all:].reshape(nb, seq, sb_heads, sb_hd))
        kmix = mix.shape[1]
        x3 = _matmul_residual(mix, w_mix, x3, mvec(i, 2), tm, _tile(d, 1024), _tile(kmix, 1024), "mixer_out_proj")

        if i % 2 == 0:
            h = _modulate_call(x3, mvec(i, 3), mvec(i, 4), BF16).reshape(m, d)
            nt = m // tm
            act = _swiglu_call(h, ffn_w_gate[j][None], ffn_w_up[j][None], jnp.zeros((nt,), jnp.int32),
                               jnp.ones((nt,), jnp.int32), tm, _tile(d_ff, 256), "ffn_up")
            x3 = _matmul_residual(act, ffn_w_down[j], x3, mvec(i, 5), tm, _tile(d, 1024), _tile(d_ff, 1024), "ffn_down")
        else:
            h32, route = _modulate_router_call(x3, mvec(i, 3), mvec(i, 4), moe_w_router[j])
            tme = _tile(m, 512, tg)
            src_rows, row_gate, tile_expert, tile_valid, pos1, pos2 = _route(route.reshape(m, LANES), n_experts, tme)
            hs = _gather_cast_call(h32.reshape(m, d), src_rows, tme)
            act = _swiglu_call(hs, moe_w_gate[j], moe_w_up[j], tile_expert, tile_valid, tme, _tile(d_ff, 256), "moe_up")
            ys = _matmul_rowscale(act, moe_w_down[j], row_gate, tile_expert, tile_valid, tme,
                                  _tile(d, 2048), _tile(d_ff, 1024), "moe_down")
            x3 = _combine_call(ys, pos1, pos2, x3, mvec(i, 5), _tile(m, 256, tg))

    y = _final_norm_call(x3.reshape(m, d), final_norm_w)
    y_prompt = y[:mp].reshape(bp, lp, d)
    y_sample = y[mp:].reshape(bs, ls, d)
    return (y_prompt, y_sample, jnp.stack(ssm_p), jnp.stack(conv_p), jnp.stack(k_p), jnp.stack(v_p),
            jnp.stack(ssm_s), jnp.stack(conv_s), jnp.stack(k_s), jnp.stack(v_s))
```

```python
import functools
import math

import numpy as np
import jax
import jax.numpy as jnp
from jax import lax
from jax.experimental import pallas as pl
from jax.experimental.pallas import tpu as pltpu

F32 = jnp.float32
BF16 = jnp.bfloat16
EPS = 1e-6
SSD_CHUNK = 64
SB_BLOCK = 128
TOP_K = 2
LANES = 128
SUBLANES = 8
VMEM_LIMIT = 56 * 1024 * 1024


def _tile(n, pref, align=LANES):
    t = min(pref, n)
    t -= t % align
    while t >= align:
        if n % t == 0:
            return t
        t -= align
    return n


def _params(*sem):
    return pltpu.CompilerParams(dimension_semantics=sem, vmem_limit_bytes=VMEM_LIMIT)


def _sigmoid(x):
    return 1.0 / (1.0 + jnp.exp(-x))


def _silu(x):
    return x * _sigmoid(x)


def _softplus(x):
    return jnp.maximum(x, 0.0) + jnp.log1p(jnp.exp(-jnp.abs(x)))


def _split_bf16(x, parts):
    out = []
    r = x
    for _ in range(parts - 1):
        p = r.astype(BF16)
        out.append(p)
        r = r - p.astype(F32)
    out.append(r.astype(BF16))
    return out


def _dot(a, b):
    return jnp.dot(a, b, preferred_element_type=F32)


def _dot_nt(a, b):
    return lax.dot_general(a, b, (((1,), (1,)), ((), ())), preferred_element_type=F32)


def _sel_dot(x, onehot, parts=3):
    acc = None
    for p in _split_bf16(x, parts):
        t = _dot(p, onehot)
        acc = t if acc is None else acc + t
    return acc


def _sel_dot_left(onehot, x, parts=3):
    acc = None
    for p in _split_bf16(x, parts):
        t = _dot(onehot, p)
        acc = t if acc is None else acc + t
    return acc


def _mod_kernel(c_ref, w_ref, b_ref, o_ref):
    cs = _silu(c_ref[...]).astype(BF16)
    o_ref[0] = _dot(cs, w_ref[0].astype(BF16)) + b_ref[0]


def _mod_call(c_pad, w_mod, b_mod):
    depth, d, n = w_mod.shape
    rows = c_pad.shape[0]
    tn = _tile(n, 512)
    return pl.pallas_call(
        _mod_kernel,
        grid=(depth, n // tn),
        in_specs=[pl.BlockSpec((rows, d), lambda i, j: (0, 0)),
                  pl.BlockSpec((1, d, tn), lambda i, j: (i, 0, j)),
                  pl.BlockSpec((1, 1, tn), lambda i, j: (i, 0, j))],
        out_specs=pl.BlockSpec((1, rows, tn), lambda i, j: (i, 0, j)),
        out_shape=jax.ShapeDtypeStruct((depth, rows, n), F32),
        compiler_params=_params("arbitrary", "arbitrary"),
        name="adaln_mod",
    )(c_pad, w_mod, b_mod.reshape(depth, 1, n))


def _modulate_kernel(x_ref, sh_ref, sc_ref, o_ref):
    x = x_ref[...]
    ms = jnp.mean(x * x, axis=-1, keepdims=True)
    h = x * lax.rsqrt(ms + EPS) * (1.0 + sc_ref[...]) + sh_ref[...]
    o_ref[...] = h.astype(o_ref.dtype)


def _modulate_call(x3, sh, sc, out_dtype):
    g, tg, d = x3.shape
    gb = _tile(g, max(1, 256 // tg), 1)
    return pl.pallas_call(
        _modulate_kernel,
        grid=(g // gb,),
        in_specs=[pl.BlockSpec((gb, tg, d), lambda i: (i, 0, 0)),
                  pl.BlockSpec((gb, 1, d), lambda i: (i, 0, 0)),
                  pl.BlockSpec((gb, 1, d), lambda i: (i, 0, 0))],
        out_specs=pl.BlockSpec((gb, tg, d), lambda i: (i, 0, 0)),
        out_shape=jax.ShapeDtypeStruct((g, tg, d), out_dtype),
        compiler_params=_params("arbitrary"),
        name="adaln_modulate",
    )(x3, sh, sc)


def _modulate_router_kernel(x_ref, sh_ref, sc_ref, wr_ref, o_ref, r_ref, *, n_experts):
    x = x_ref[...]
    gb, tg, d = x.shape
    ms = jnp.mean(x * x, axis=-1, keepdims=True)
    h = x * lax.rsqrt(ms + EPS) * (1.0 + sc_ref[...]) + sh_ref[...]
    o_ref[...] = h
    h2 = h.reshape(gb * tg, d)
    h_hi, h_lo = _split_bf16(h2, 2)
    w_hi, w_lo = _split_bf16(wr_ref[...], 2)
    logits = _dot(h_hi, w_hi) + _dot(h_hi, w_lo) + _dot(h_lo, w_hi)
    lane = lax.broadcasted_iota(jnp.int32, logits.shape, 1).astype(F32)
    neg = jnp.float32(-jnp.inf)
    lg = jnp.where(lane < n_experts, logits, neg)
    m1 = jnp.max(lg, axis=1, keepdims=True)
    i1 = jnp.min(jnp.where(lg == m1, lane, float(LANES)), axis=1, keepdims=True)
    lg2 = jnp.where(lane == i1, neg, lg)
    m2 = jnp.max(lg2, axis=1, keepdims=True)
    i2 = jnp.min(jnp.where(lg2 == m2, lane, float(LANES)), axis=1, keepdims=True)
    e2 = jnp.exp(m2 - m1)
    g1 = 1.0 / (1.0 + e2)
    g2 = e2 / (1.0 + e2)
    out = jnp.where(lane == 0.0, i1,
                    jnp.where(lane == 1.0, i2,
                              jnp.where(lane == 2.0, g1, jnp.where(lane == 3.0, g2, 0.0))))
    r_ref[...] = out.reshape(gb, tg, LANES)


def _modulate_router_call(x3, sh, sc, w_router):
    g, tg, d = x3.shape
    n_experts = w_router.shape[1]
    wr = jnp.zeros((d, LANES), F32).at[:, :n_experts].set(w_router)
    gb = _tile(g, max(1, 256 // tg), 1)
    return pl.pallas_call(
        functools.partial(_modulate_router_kernel, n_experts=n_experts),
        grid=(g // gb,),
        in_specs=[pl.BlockSpec((gb, tg, d), lambda i: (i, 0, 0)),
                  pl.BlockSpec((gb, 1, d), lambda i: (i, 0, 0)),
                  pl.BlockSpec((gb, 1, d), lambda i: (i, 0, 0)),
                  pl.BlockSpec((d, LANES), lambda i: (0, 0))],
        out_specs=[pl.BlockSpec((gb, tg, d), lambda i: (i, 0, 0)),
                   pl.BlockSpec((gb, tg, LANES), lambda i: (i, 0, 0))],
        out_shape=[jax.ShapeDtypeStruct((g, tg, d), F32),
                   jax.ShapeDtypeStruct((g, tg, LANES), F32)],
        compiler_params=_params("arbitrary"),
        name="adaln_modulate_router",
    )(x3, sh, sc, wr)


def _ws_kernel(a_ref, w_ref, *rest):
    o_refs, wb_ref = rest[:-1], rest[-1]

    @pl.when(pl.program_id(1) == 0)
    def _():
        wb_ref[...] = w_ref[...].astype(BF16)

    acc = _dot(a_ref[...], wb_ref[...])
    for o_ref in o_refs:
        o_ref[...] = acc.astype(o_ref.dtype)


def _matmul_ws(a, w, col_off, n_cols, tn, tm, name, out_dtypes=(F32,)):
    m, k = a.shape
    off = col_off // tn
    assert col_off % tn == 0 and n_cols % tn == 0 and m % tm == 0
    return pl.pallas_call(
        _ws_kernel,
        grid=(n_cols // tn, m // tm),
        in_specs=[pl.BlockSpec((tm, k), lambda n, i: (i, 0)),
                  pl.BlockSpec((k, tn), lambda n, i: (0, n + off))],
        out_specs=[pl.BlockSpec((tm, tn), lambda n, i: (i, n)) for _ in out_dtypes],
        out_shape=[jax.ShapeDtypeStruct((m, n_cols), dt) for dt in out_dtypes],
        scratch_shapes=[pltpu.VMEM((k, tn), BF16)],
        compiler_params=_params("arbitrary", "arbitrary"),
        name=name,
    )(a, w)


def _swiglu_kernel(te_ref, tv_ref, a_ref, wg_ref, wu_ref, o_ref, wgb_ref, wub_ref):
    i = pl.program_id(1)
    fresh = jnp.logical_or(i == 0, te_ref[i] != te_ref[jnp.maximum(i - 1, 0)])

    @pl.when(fresh)
    def _():
        wgb_ref[...] = wg_ref[0].astype(BF16)
        wub_ref[...] = wu_ref[0].astype(BF16)

    @pl.when(tv_ref[i] != 0)
    def _():
        a = a_ref[...]
        g = _dot(a, wgb_ref[...])
        u = _dot(a, wub_ref[...])
        o_ref[...] = (_silu(g) * u).astype(o_ref.dtype)

    @pl.when(tv_ref[i] == 0)
    def _():
        o_ref[...] = jnp.zeros_like(o_ref)


def _swiglu_call(a, w_gate, w_up, tile_expert, tile_valid, tm, tf, name):
    m, k = a.shape
    _, _, f = w_gate.shape
    nt = m // tm
    grid_spec = pltpu.PrefetchScalarGridSpec(
        num_scalar_prefetch=2,
        grid=(f // tf, nt),
        in_specs=[pl.BlockSpec((tm, k), lambda j, i, te, tv: (i, 0)),
                  pl.BlockSpec((1, k, tf), lambda j, i, te, tv: (te[i], 0, j)),
                  pl.BlockSpec((1, k, tf), lambda j, i, te, tv: (te[i], 0, j))],
        out_specs=pl.BlockSpec((tm, tf), lambda j, i, te, tv: (i, j)),
        scratch_shapes=[pltpu.VMEM((k, tf), BF16), pltpu.VMEM((k, tf), BF16)],
    )
    return pl.pallas_call(
        _swiglu_kernel,
        grid_spec=grid_spec,
        out_shape=jax.ShapeDtypeStruct((m, f), BF16),
        compiler_params=_params("arbitrary", "arbitrary"),
        name=name,
    )(tile_expert, tile_valid, a, w_gate, w_up)


def _kt_accumulate(tv_ref, a_ref, w_ref, acc_ref):
    k = pl.program_id(2)

    @pl.when(k == 0)
    def _():
        acc_ref[...] = jnp.zeros_like(acc_ref)

    @pl.when(tv_ref[pl.program_id(0)] != 0)
    def _():
        acc_ref[...] += _dot(a_ref[...], w_ref[0].astype(BF16))


def _kt_residual_kernel(te_ref, tv_ref, a_ref, w_ref, x_ref, g_ref, o_ref, acc_ref):
    _kt_accumulate(tv_ref, a_ref, w_ref, acc_ref)

    @pl.when(pl.program_id(2) == pl.num_programs(2) - 1)
    def _():
        o_ref[...] = x_ref[...] + g_ref[...] * acc_ref[...].reshape(o_ref.shape)


def _kt_rowscale_kernel(te_ref, tv_ref, a_ref, w_ref, rs_ref, o_ref, acc_ref):
    _kt_accumulate(tv_ref, a_ref, w_ref, acc_ref)

    @pl.when(pl.program_id(2) == pl.num_programs(2) - 1)
    def _():
        o_ref[...] = rs_ref[0] * acc_ref[...]


def _matmul_residual(a, w, x3, gate3, tm, tn, tk, name):
    m, k = a.shape
    n = w.shape[-1]
    g, tg, _ = x3.shape
    gb = tm // tg
    nt = m // tm
    te = jnp.zeros((nt,), jnp.int32)
    tv = jnp.ones((nt,), jnp.int32)
    grid_spec = pltpu.PrefetchScalarGridSpec(
        num_scalar_prefetch=2,
        grid=(nt, n // tn, k // tk),
        in_specs=[pl.BlockSpec((tm, tk), lambda i, j, kk, te, tv: (i, kk)),
                  pl.BlockSpec((1, tk, tn), lambda i, j, kk, te, tv: (te[i], kk, j)),
                  pl.BlockSpec((gb, tg, tn), lambda i, j, kk, te, tv: (i, 0, j)),
                  pl.BlockSpec((gb, 1, tn), lambda i, j, kk, te, tv: (i, 0, j))],
        out_specs=pl.BlockSpec((gb, tg, tn), lambda i, j, kk, te, tv: (i, 0, j)),
        scratch_shapes=[pltpu.VMEM((tm, tn), F32)],
    )
    return pl.pallas_call(
        _kt_residual_kernel,
        grid_spec=grid_spec,
        out_shape=jax.ShapeDtypeStruct(x3.shape, F32),
        compiler_params=_params("arbitrary", "arbitrary", "arbitrary"),
        name=name,
    )(te, tv, a, w.reshape((1,) + w.shape[-2:]), x3, gate3)


def _matmul_rowscale(a, w, rowscale, tile_expert, tile_valid, tm, tn, tk, name):
    m, k = a.shape
    n = w.shape[-1]
    nt = m // tm
    grid_spec = pltpu.PrefetchScalarGridSpec(
        num_scalar_prefetch=2,
        grid=(nt, n // tn, k // tk),
        in_specs=[pl.BlockSpec((tm, tk), lambda i, j, kk, te, tv: (i, kk)),
                  pl.BlockSpec((1, tk, tn), lambda i, j, kk, te, tv: (te[i], kk, j)),
                  pl.BlockSpec((1, tm, 1), lambda i, j, kk, te, tv: (i, 0, 0))],
        out_specs=pl.BlockSpec((tm, tn), lambda i, j, kk, te, tv: (i, j)),
        scratch_shapes=[pltpu.VMEM((tm, tn), F32)],
    )
    return pl.pallas_call(
        _kt_rowscale_kernel,
        grid_spec=grid_spec,
        out_shape=jax.ShapeDtypeStruct((m, n), F32),
        compiler_params=_params("arbitrary", "arbitrary", "arbitrary"),
        name=name,
    )(tile_expert, tile_valid, a, w, rowscale.reshape(nt, tm, 1))


def _conv_kernel(x_ref, prev_ref, w_ref, b_ref, o_ref, xp_ref, *, tl, taps):
    t = pl.program_id(2)
    hist = taps - 1
    top = SUBLANES - hist

    @pl.when(t == 0)
    def _():
        xp_ref[top:SUBLANES, :] = prev_ref[0]

    @pl.when(t > 0)
    def _():
        xp_ref[0:SUBLANES, :] = xp_ref[tl:tl + SUBLANES, :]

    x = x_ref[...]
    xp_ref[SUBLANES:SUBLANES + tl, :] = x
    w = w_ref[...]
    acc = b_ref[...] + w[hist:taps] * x
    for k in range(hist):
        acc = acc + w[k:k + 1] * xp_ref[top + k:top + k + tl, :]
    o_ref[...] = _silu(acc)


def _conv_call(proj, row0, nb, seq, col0, conv_prev, conv_w, conv_b, name):
    taps, c = conv_w.shape
    tl = _tile(seq, 512, SUBLANES)
    tc = _tile(math.gcd(c, col0) if col0 else c, 1024)
    nt = seq // tl
    assert row0 % tl == 0 and col0 % tc == 0
    rb0, cb0 = row0 // tl, col0 // tc
    return pl.pallas_call(
        functools.partial(_conv_kernel, tl=tl, taps=taps),
        grid=(nb, c // tc, nt),
        in_specs=[pl.BlockSpec((tl, tc), lambda b, j, t: (rb0 + b * nt + t, cb0 + j)),
                  pl.BlockSpec((1, taps - 1, tc), lambda b, j, t: (b, 0, j)),
                  pl.BlockSpec((taps, tc), lambda b, j, t: (0, j)),
                  pl.BlockSpec((1, tc), lambda b, j, t: (0, j))],
        out_specs=pl.BlockSpec((tl, tc), lambda b, j, t: (b * nt + t, j)),
        out_shape=jax.ShapeDtypeStruct((nb * seq, c), F32),
        scratch_shapes=[pltpu.VMEM((tl + SUBLANES, tc), F32)],
        compiler_params=_params("arbitrary", "arbitrary", "arbitrary"),
        name=name,
    )(proj, conv_prev, conv_w, conv_b.reshape(1, c))


def _ssd_kernel(x_ref, b_ref, c_ref, dtr_ref, z_ref, sel_ref, e_ref, bias_ref, alog_ref, dskip_ref, nw_ref,
                s0_ref, g_ref, s_ref, st_ref, *, q, heads, hdim):
    c = pl.program_id(2)
    rp = heads * hdim
    per_lane = LANES // hdim

    @pl.when(c == 0)
    def _():
        st_ref[...] = s0_ref[0, 0].T

    dt_all = _softplus(dtr_ref[...] + bias_ref[...])
    a_all = -jnp.exp(alog_ref[...])
    sel = sel_ref[0]
    dtg = _sel_dot(dt_all, sel)
    dag = _sel_dot(dt_all * a_all, sel)
    ri = lax.broadcasted_iota(jnp.int32, (q, q), 0)
    ci = lax.broadcasted_iota(jnp.int32, (q, q), 1)
    causal = ri >= ci
    tril = jnp.where(causal, 1.0, 0.0).astype(BF16)
    acs = _sel_dot_left(tril, dag)
    acs_t = acs.T
    expand = e_ref[...]
    dt_exp = _sel_dot(dtg, expand)
    acs_exp = _sel_dot(acs, expand)

    x = x_ref[...]
    xdt = x * dt_exp
    bm = b_ref[...].astype(BF16)
    cm = c_ref[...].astype(BF16)
    cb = _dot_nt(cm, bm)

    st = st_ref[...]
    y = _dot(cm, st.astype(BF16)) * jnp.exp(acs_exp)
    last = acs_exp[q - 1:q, :]
    xs = (xdt * jnp.exp(last - acs_exp)).astype(BF16)
    bt = b_ref[...].T.astype(BF16)
    st_ref[...] = st * jnp.exp(last) + _dot(bt, xs)

    lane = lax.broadcasted_iota(jnp.int32, (q, LANES), 1)
    xdt_b = xdt
    pieces = []
    for pr in range(heads // per_lane):
        xp = xdt_b[:, pr * LANES:(pr + 1) * LANES]
        acc = None
        for hh in range(per_lane):
            r = pr * per_lane + hh
            seg = acs[:, r:r + 1] - acs_t[r:r + 1, :]
            dec = jnp.exp(jnp.where(causal, seg, -jnp.inf))
            m = (cb * dec).astype(BF16)
            in_head = jnp.logical_and(lane >= hh * hdim, lane < (hh + 1) * hdim)
            xm = jnp.where(in_head, xp, 0.0).astype(BF16)
            t = _dot(m, xm)
            acc = t if acc is None else acc + t
        pieces.append(acc)
    y = y + jnp.concatenate(pieces, axis=1) + dskip_ref[...] * x

    gt = y * _silu(z_ref[...])
    ms = jnp.mean(gt * gt, axis=-1, keepdims=True)
    g_ref[...] = (gt * lax.rsqrt(ms + EPS) * nw_ref[...]).astype(g_ref.dtype)

    @pl.when(c == pl.num_programs(2) - 1)
    def _():
        s_ref[0, 0] = st_ref[...].T


def _ssd_call(xbc, proj, dt_raw, row0, nb, seq, s0, consts, name):
    sel, expand, bias, alog, dskip, nw = consts
    groups, n_heads, _ = sel.shape
    _, _, rp, n = s0.shape
    heads = n_heads // groups
    hdim = rp // heads
    inner = groups * rp
    q = SSD_CHUNK if seq % SSD_CHUNK == 0 else seq
    nc = seq // q
    assert row0 % q == 0 and LANES % hdim == 0 and heads % (LANES // hdim) == 0 and heads <= LANES
    rb0 = row0 // q
    nb_cols = inner // n
    return pl.pallas_call(
        functools.partial(_ssd_kernel, q=q, heads=heads, hdim=hdim),
        grid=(nb, groups, nc),
        in_specs=[pl.BlockSpec((q, rp), lambda b, g, c: (b * nc + c, g)),
                  pl.BlockSpec((q, n), lambda b, g, c: (b * nc + c, nb_cols + g)),
                  pl.BlockSpec((q, n), lambda b, g, c: (b * nc + c, nb_cols + groups + g)),
                  pl.BlockSpec((q, n_heads), lambda b, g, c: (rb0 + b * nc + c, 0)),
                  pl.BlockSpec((q, rp), lambda b, g, c: (rb0 + b * nc + c, g)),
                  pl.BlockSpec((1, n_heads, LANES), lambda b, g, c: (g, 0, 0)),
                  pl.BlockSpec((LANES, rp), lambda b, g, c: (0, 0)),
                  pl.BlockSpec((1, n_heads), lambda b, g, c: (0, 0)),
                  pl.BlockSpec((1, n_heads), lambda b, g, c: (0, 0)),
                  pl.BlockSpec((1, rp), lambda b, g, c: (0, g)),
                  pl.BlockSpec((1, rp), lambda b, g, c: (0, g)),
                  pl.BlockSpec((1, 1, rp, n), lambda b, g, c: (b, g, 0, 0))],
        out_specs=[pl.BlockSpec((q, rp), lambda b, g, c: (b * nc + c, g)),
                   pl.BlockSpec((1, 1, rp, n), lambda b, g, c: (b, g, 0, 0))],
        out_shape=[jax.ShapeDtypeStruct((nb * seq, inner), BF16),
                   jax.ShapeDtypeStruct(s0.shape, F32)],
        scratch_shapes=[pltpu.VMEM((n, rp), F32)],
        compiler_params=_params("arbitrary", "arbitrary", "arbitrary"),
        name=name,
    )(xbc, xbc, xbc, dt_raw, proj, sel, expand, bias, alog, dskip, nw, s0)


def _sb_weights(z, carry, suffix, strict_mask):
    sub = suffix.shape[0]
    tail = jnp.log(1.0 + jnp.exp(-jnp.abs(z)))
    log_beta = jnp.minimum(z, 0.0) - tail
    log_keep = jnp.minimum(-z, 0.0) - tail
    if strict_mask is not None:
        log_keep = jnp.where(strict_mask, log_keep, 0.0)
    nsub = z.shape[1] // sub
    later = [None] * nsub
    run = carry
    for j in reversed(range(nsub)):
        lk = log_keep[:, j * sub:(j + 1) * sub]
        later[j] = _sel_dot(lk, suffix, parts=2) + run
        run = run + jnp.sum(lk, axis=1, keepdims=True)
    later = later[0] if nsub == 1 else jnp.concatenate(later, axis=1)
    w = jnp.exp(log_beta + later)
    if strict_mask is not None:
        w = jnp.where(strict_mask, w, 0.0)
    return w, run


def _suffix_matrix(n):
    ri = lax.broadcasted_iota(jnp.int32, (n, n), 0)
    ci = lax.broadcasted_iota(jnp.int32, (n, n), 1)
    return jnp.where(ri > ci, 1.0, 0.0).astype(BF16)


def _stacked_strict_mask(nh, t):
    ri = lax.broadcasted_iota(jnp.int32, (nh * t, t), 0)
    ci = lax.broadcasted_iota(jnp.int32, (nh * t, t), 1)
    return ci < jnp.bitwise_and(ri, t - 1)


def _sb_scores(q_ref, load_k, nh, hd, scale, h0=0):
    zs = [_dot_nt(q_ref[:, (h0 + h) * hd:(h0 + h + 1) * hd], load_k(h0 + h)) for h in range(nh)]
    return jnp.concatenate(zs, axis=0) * scale


def _sb_values(w, load_v, nh, t, h0=0):
    wb = w.astype(BF16)
    return jnp.concatenate([_dot(wb[h * t:(h + 1) * t], load_v(h0 + h)) for h in range(nh)], axis=0)


def _sb_prompt_kernel(q_ref, k_ref, v_ref, o_ref, acc_ref, carry_ref, *, blk, nh, hd, scale):
    qi = pl.program_id(2)
    suffix = _suffix_matrix(blk)

    def visit(kb, carry, mask):
        ks = pl.multiple_of(kb * blk, blk)
        z = _sb_scores(q_ref, lambda h: k_ref[pl.ds(ks, blk), h * hd:(h + 1) * hd], nh, hd, scale)
        w, run = _sb_weights(z, carry, suffix, mask)
        return _sb_values(w, lambda h: v_ref[pl.ds(ks, blk), h * hd:(h + 1) * hd], nh, blk), run

    out, run = visit(qi, jnp.zeros((nh * blk, 1), F32), _stacked_strict_mask(nh, blk))
    acc_ref[...] = out
    carry_ref[...] = run

    def body(i, c):
        out, run = visit(qi - 1 - i, carry_ref[...], None)
        acc_ref[...] += out
        carry_ref[...] = run
        return c

    lax.fori_loop(0, qi, body, 0)
    for h in range(nh):
        o_ref[:, h * hd:(h + 1) * hd] = acc_ref[h * blk:(h + 1) * blk, :].astype(o_ref.dtype)


def _heads_per_step(n_heads):
    return SUBLANES if n_heads % SUBLANES == 0 else n_heads


def _sb_prompt_call(q16, kv16, row0, nb, seq, n_heads, hd, scale, name):
    blk = SB_BLOCK
    nq = seq // blk
    nh = _heads_per_step(n_heads)
    ng = n_heads // nh
    assert row0 % seq == 0 and hd % LANES == 0 and blk & (blk - 1) == 0
    sb0 = row0 // seq
    rb0 = row0 // blk
    return pl.pallas_call(
        functools.partial(_sb_prompt_kernel, blk=blk, nh=nh, hd=hd, scale=scale),
        grid=(nb, ng, nq),
        in_specs=[pl.BlockSpec((blk, nh * hd), lambda b, g, i: (rb0 + b * nq + i, g)),
                  pl.BlockSpec((seq, nh * hd), lambda b, g, i: (sb0 + b, g)),
                  pl.BlockSpec((seq, nh * hd), lambda b, g, i: (sb0 + b, ng + g))],
        out_specs=pl.BlockSpec((blk, nh * hd), lambda b, g, i: (b * nq + i, g)),
        out_shape=jax.ShapeDtypeStruct((nb * seq, n_heads * hd), BF16),
        scratch_shapes=[pltpu.VMEM((nh * blk, hd), F32), pltpu.VMEM((nh * blk, 1), F32)],
        compiler_params=_params("arbitrary", "arbitrary", "arbitrary"),
        name=name,
    )(q16, kv16, kv16)


def _sb_decode_kernel(q_ref, kn_ref, vn_ref, kc_ref, vc_ref, o_ref, acc_ref, carry_ref, *, nh, hd, sub, scale):
    p = pl.program_id(2)
    t = q_ref.shape[0]

    @pl.when(p == 0)
    def _():
        z = _sb_scores(q_ref, lambda h: kn_ref[:, h * hd:(h + 1) * hd], nh, hd, scale)
        w, run = _sb_weights(z, jnp.zeros((nh * t, 1), F32), _suffix_matrix(t), _stacked_strict_mask(nh, t))
        acc_ref[...] = _sb_values(w, lambda h: vn_ref[:, h * hd:(h + 1) * hd], nh, t)
        carry_ref[...] = run

    pb = kc_ref.shape[0]
    k_rows = kc_ref.reshape(pb * nh, hd)
    v_rows = vc_ref.reshape(pb * nh, hd)
    z = _sb_scores(q_ref, lambda h: k_rows[pl.ds(h, pb, stride=nh), :].astype(BF16), nh, hd, scale)
    w, run = _sb_weights(z, carry_ref[...], _suffix_matrix(sub), None)
    acc_ref[...] += _sb_values(w, lambda h: v_rows[pl.ds(h, pb, stride=nh), :].astype(BF16), nh, t)
    carry_ref[...] = run

    @pl.when(p == pl.num_programs(2) - 1)
    def _():
        for h in range(nh):
            o_ref[:, h * hd:(h + 1) * hd] = acc_ref[h * t:(h + 1) * t, :].astype(o_ref.dtype)


def _sb_decode_call(q16, kv16, row0, nb, seq, n_heads, hd, cache_k, cache_v, scale, name):
    past = cache_k.shape[1]
    nh = _heads_per_step(n_heads)
    ng = n_heads // nh
    pb = _tile(past, 512)
    sub = _tile(pb, 256)
    npb = past // pb
    assert row0 % seq == 0 and seq & (seq - 1) == 0
    sb0 = row0 // seq
    return pl.pallas_call(
        functools.partial(_sb_decode_kernel, nh=nh, hd=hd, sub=sub, scale=scale),
        grid=(nb, ng, npb),
        in_specs=[pl.BlockSpec((seq, nh * hd), lambda b, g, p: (sb0 + b, g)),
                  pl.BlockSpec((seq, nh * hd), lambda b, g, p: (sb0 + b, g)),
                  pl.BlockSpec((seq, nh * hd), lambda b, g, p: (sb0 + b, ng + g)),
                  pl.BlockSpec((None, pb, nh, hd), lambda b, g, p: (b, npb - 1 - p, g, 0)),
                  pl.BlockSpec((None, pb, nh, hd), lambda b, g, p: (b, npb - 1 - p, g, 0))],
        out_specs=pl.BlockSpec((seq, nh * hd), lambda b, g, p: (b, g)),
        out_shape=jax.ShapeDtypeStruct((nb * seq, n_heads * hd), BF16),
        scratch_shapes=[pltpu.VMEM((nh * seq, hd), F32), pltpu.VMEM((nh * seq, 1), F32)],
        compiler_params=_params("arbitrary", "arbitrary", "arbitrary"),
        name=name,
    )(q16, kv16, kv16, cache_k, cache_v)


def _gather_cast_kernel(src_ref, h_ref, o_ref, buf_ref, sem):
    tm = buf_ref.shape[0]

    def row_copy(r):
        return pltpu.make_async_copy(h_ref.at[pl.ds(src_ref[0, 0, r], 1), :], buf_ref.at[pl.ds(r, 1), :], sem)

    def start(r, carry):
        row_copy(r).start()
        return carry

    def wait(r, carry):
        row_copy(r).wait()
        return carry

    lax.fori_loop(0, tm, start, 0)
    lax.fori_loop(0, tm, wait, 0)
    o_ref[...] = buf_ref[...].astype(o_ref.dtype)


def _gather_cast_call(h, src_rows, tm):
    m, d = h.shape
    nt = src_rows.shape[0] // tm
    return pl.pallas_call(
        _gather_cast_kernel,
        grid=(nt,),
        in_specs=[pl.BlockSpec((1, 1, tm), lambda i: (i, 0, 0), memory_space=pltpu.SMEM),
                  pl.BlockSpec(memory_space=pl.ANY)],
        out_specs=pl.BlockSpec((tm, d), lambda i: (i, 0)),
        out_shape=jax.ShapeDtypeStruct((nt * tm, d), BF16),
        scratch_shapes=[pltpu.VMEM((tm, d), F32), pltpu.SemaphoreType.DMA(())],
        compiler_params=_params("arbitrary"),
        name="moe_gather_rows",
    )(src_rows.reshape(nt, 1, tm), h)


def _combine_kernel(p1_ref, p2_ref, ys_ref, x_ref, g_ref, o_ref, buf_ref, sem):
    tm = buf_ref.shape[1]

    def copies(r):
        return (pltpu.make_async_copy(ys_ref.at[pl.ds(p1_ref[0, 0, r], 1), :], buf_ref.at[0, pl.ds(r, 1), :], sem),
                pltpu.make_async_copy(ys_ref.at[pl.ds(p2_ref[0, 0, r], 1), :], buf_ref.at[1, pl.ds(r, 1), :], sem))

    def start(r, carry):
        for cp in copies(r):
            cp.start()
        return carry

    def wait(r, carry):
        for cp in copies(r):
            cp.wait()
        return carry

    lax.fori_loop(0, tm, start, 0)
    lax.fori_loop(0, tm, wait, 0)
    f = (buf_ref[0] + buf_ref[1]).reshape(o_ref.shape)
    o_ref[...] = x_ref[...] + g_ref[...] * f


def _combine_call(ys, pos1, pos2, x3, gate3, tm):
    g, tg, d = x3.shape
    gb = tm // tg
    nt = (g * tg) // tm
    return pl.pallas_call(
        _combine_kernel,
        grid=(nt,),
        in_specs=[pl.BlockSpec((1, 1, tm), lambda i: (i, 0, 0), memory_space=pltpu.SMEM),
                  pl.BlockSpec((1, 1, tm), lambda i: (i, 0, 0), memory_space=pltpu.SMEM),
                  pl.BlockSpec(memory_space=pl.ANY),
                  pl.BlockSpec((gb, tg, d), lambda i: (i, 0, 0)),
                  pl.BlockSpec((gb, 1, d), lambda i: (i, 0, 0))],
        out_specs=pl.BlockSpec((gb, tg, d), lambda i: (i, 0, 0)),
        out_shape=jax.ShapeDtypeStruct(x3.shape, F32),
        scratch_shapes=[pltpu.VMEM((2, tm, d), F32), pltpu.SemaphoreType.DMA(())],
        compiler_params=_params("arbitrary"),
        name="moe_combine_rows",
    )(pos1.reshape(nt, 1, tm), pos2.reshape(nt, 1, tm), ys, x3, gate3)


def _final_norm_kernel(x_ref, w_ref, o_ref):
    x = x_ref[...]
    ms = jnp.mean(x * x, axis=-1, keepdims=True)
    o_ref[...] = x * lax.rsqrt(ms + EPS) * w_ref[...]


def _final_norm_call(x2, w, row0, n_rows, name):
    _, d = x2.shape
    tm = _tile(math.gcd(n_rows, row0) if row0 else n_rows, 256, SUBLANES)
    rb0 = row0 // tm
    return pl.pallas_call(
        _final_norm_kernel,
        grid=(n_rows // tm,),
        in_specs=[pl.BlockSpec((tm, d), lambda i: (rb0 + i, 0)), pl.BlockSpec((1, d), lambda i: (0, 0))],
        out_specs=pl.BlockSpec((tm, d), lambda i: (i, 0)),
        out_shape=jax.ShapeDtypeStruct((n_rows, d), F32),
        compiler_params=_params("arbitrary"),
        name=name,
    )(x2, w.reshape(1, d))


def _ssd_constants(groups, n_heads, hdim, dt_bias, a_log, d_skip, norm_w):
    heads = n_heads // groups
    sel = np.zeros((groups, n_heads, LANES), np.float32)
    for g in range(groups):
        for r in range(heads):
            sel[g, g * heads + r, r] = 1.0
    expand = np.zeros((LANES, heads * hdim), np.float32)
    for r in range(heads):
        expand[r, r * hdim:(r + 1) * hdim] = 1.0
    return (jnp.asarray(sel, BF16), jnp.asarray(expand, BF16),
            dt_bias.astype(F32).reshape(1, n_heads), a_log.astype(F32).reshape(1, n_heads),
            jnp.repeat(d_skip.astype(F32), hdim).reshape(1, n_heads * hdim), norm_w.reshape(1, n_heads * hdim))


def _route(route, n_experts, tm):
    m = route.shape[0]
    idx = route[:, :TOP_K].astype(jnp.int32)
    gates = route[:, TOP_K:2 * TOP_K]
    flat_e = idx.reshape(-1)
    n_assign = flat_e.shape[0]
    n_tiles = n_assign // tm + n_experts
    order = jnp.argsort(flat_e, stable=True)
    counts = jnp.zeros((n_experts,), jnp.int32).at[flat_e].add(1)
    padded = ((counts + tm - 1) // tm) * tm
    pad_end = jnp.cumsum(padded)
    pad_start = pad_end - padded
    cnt_start = jnp.cumsum(counts) - counts
    sorted_e = flat_e[order]
    dest_sorted = pad_start[sorted_e] + (jnp.arange(n_assign, dtype=jnp.int32) - cnt_start[sorted_e])
    dest = jnp.zeros((n_assign,), jnp.int32).at[order].set(dest_sorted)
    src_rows = jnp.zeros((n_tiles * tm,), jnp.int32).at[dest].set(jnp.arange(n_assign, dtype=jnp.int32) // TOP_K)
    row_gate = jnp.zeros((n_tiles * tm,), F32).at[dest].set(gates.reshape(-1))
    tile_start = jnp.arange(n_tiles, dtype=jnp.int32) * tm
    tile_expert = jnp.minimum(jnp.searchsorted(pad_end, tile_start, side="right"), n_experts - 1).astype(jnp.int32)
    tile_valid = (tile_start < pad_end[-1]).astype(jnp.int32)
    dest2 = dest.reshape(m, TOP_K)
    return src_rows, row_gate, tile_expert, tile_valid, dest2[:, 0], dest2[:, 1]


def kernel(x_prompt, x_sample, state_ssm, state_conv, cache_k, cache_v, c_prompt, c_sample, w_mod, b_mod, ssd_w_in, ssd_conv_w, ssd_conv_b, ssd_dt_bias, ssd_a_log, ssd_d, ssd_norm_w, ssd_w_out, sb_w_qkv, sb_w_o, ffn_w_gate, ffn_w_up, ffn_w_down, moe_w_router, moe_w_gate, moe_w_up, moe_w_down, final_norm_w):
    bp, lp, d = x_prompt.shape
    bs, ls, _ = x_sample.shape
    depth = w_mod.shape[0]
    mp, msamp = bp * lp, bs * ls
    m = mp + msamp
    tg = math.gcd(lp, ls)
    assert tg % SUBLANES == 0
    n_groups = m // tg
    n_heads = ssd_a_log.shape[1]
    hdim, n_state = state_ssm.shape[3], state_ssm.shape[4]
    inner = n_heads * hdim
    conv_dim = ssd_conv_w.shape[2]
    ssd_groups = (conv_dim - inner) // (2 * n_state)
    sb_heads, sb_hd = cache_k.shape[3], cache_k.shape[4]
    n_experts = moe_w_router.shape[2]
    d_ff = ffn_w_gate.shape[2]
    tm = _tile(m, 1024, tg)

    seq_of_group = np.concatenate([np.repeat(np.arange(bp), lp // tg), bp + np.repeat(np.arange(bs), ls // tg)])
    n_seq = bp + bs
    rows = -(-n_seq // 16) * 16
    c_all = jnp.zeros((rows, d), F32).at[:n_seq].set(jnp.concatenate([c_prompt, c_sample], axis=0))
    mod = _mod_call(c_all, w_mod, b_mod)
    mod_g = mod[:, seq_of_group, :].reshape(depth, n_groups, 1, 6, d)

    def mvec(i, which):
        return mod_g[i, :, :, which, :]

    x3 = jnp.concatenate([x_prompt.reshape(mp, d), x_sample.reshape(msamp, d)], axis=0).reshape(n_groups, tg, d)
    zeros_ssm = jnp.zeros((bp,) + state_ssm.shape[2:], F32)
    zeros_conv = jnp.zeros((bp,) + state_conv.shape[2:], F32)
    ssm_p, conv_p, k_p, v_p, ssm_s, conv_s, k_s, v_s = [], [], [], [], [], [], [], []

    for i in range(depth):
        j = i // 2
        h = _modulate_call(x3, mvec(i, 0), mvec(i, 1), BF16).reshape(m, d)
        if i % 2 == 0:
            w_in = ssd_w_in[j]
            n_main = inner + conv_dim
            proj, = _matmul_ws(h, w_in, 0, n_main, _tile(n_main, 512), tm, "ssd_in_proj")
            dt_raw, = _matmul_ws(h, w_in, n_main, n_heads, n_heads, tm, "ssd_in_proj_dt")
            consts = _ssd_constants(ssd_groups, n_heads, hdim, ssd_dt_bias[j], ssd_a_log[j], ssd_d[j], ssd_norm_w[j])
            outs = []
            for (row0, nb, seq, prev, s0, conv_out, ssm_out, tag) in (
                    (0, bp, lp, zeros_conv, zeros_ssm, conv_p, ssm_p, "prompt"),
                    (mp, bs, ls, state_conv[j], state_ssm[j], conv_s, ssm_s, "sample")):
                xbc = _conv_call(proj, row0, nb, seq, inner, prev, ssd_conv_w[j], ssd_conv_b[j], "ssd_conv_" + tag)
                s0g = s0.astype(F32).reshape(nb, ssd_groups, (n_heads // ssd_groups) * hdim, n_state)
                g_out, s_fin = _ssd_call(xbc, proj, dt_raw, row0, nb, seq, s0g, consts, "ssd_scan_" + tag)
                outs.append(g_out)
                ssm_out.append(s_fin.reshape(nb, n_heads, hdim, n_state))
                hist = ssd_conv_w.shape[1] - 1
                last_rows = (row0 + np.arange(nb)[:, None] * seq + np.arange(seq - hist, seq)[None, :]).reshape(-1)
                conv_out.append(jnp.take(proj, last_rows, axis=0)[:, inner:].reshape(nb, hist, conv_dim))
            mix = jnp.concatenate(outs, axis=0)
            w_mix = ssd_w_out[j]
        else:
            hd_all = sb_heads * sb_hd
            tn = _tile(hd_all, 512)
            q16, = _matmul_ws(h, sb_w_qkv[j], 0, hd_all, tn, tm, "sb_q", (BF16,))
            kv32, kv16 = _matmul_ws(h, sb_w_qkv[j], hd_all, 2 * hd_all, tn, tm, "sb_kv", (F32, BF16))
            scale = float(sb_hd) ** -0.5
            o_p = _sb_prompt_call(q16, kv16, 0, bp, lp, sb_heads, sb_hd, scale, "sb_attn_prompt")
            o_s = _sb_decode_call(q16, kv16, mp, bs, ls, sb_heads, sb_hd, cache_k[j], cache_v[j], scale,
                                  "sb_attn_sample")
            mix = jnp.concatenate([o_p, o_s], axis=0)
            w_mix = sb_w_o[j]
            for (row0, nb, seq, k_out, v_out) in ((0, bp, lp, k_p, v_p), (mp, bs, ls, k_s, v_s)):
                blk = kv32[row0:row0 + nb * seq]
                k_out.append(blk[:, :hd_all].reshape(nb, seq, sb_heads, sb_hd))
                v_out.append(blk[:, hd_all:].reshape(nb, seq, sb_heads, sb_hd))
        kmix = mix.shape[1]
        x3 = _matmul_residual(mix, w_mix, x3, mvec(i, 2), tm, _tile(d, 1024), _tile(kmix, 1024), "mixer_out_proj")

        if i % 2 == 0:
            h = _modulate_call(x3, mvec(i, 3), mvec(i, 4), BF16).reshape(m, d)
            nt = m // tm
            act = _swiglu_call(h, ffn_w_gate[j][None], ffn_w_up[j][None], jnp.zeros((nt,), jnp.int32),
                               jnp.ones((nt,), jnp.int32), tm, _tile(d_ff, 256), "ffn_up")
            x3 = _matmul_residual(act, ffn_w_down[j], x3, mvec(i, 5), tm, _tile(d, 1024), _tile(d_ff, 1024), "ffn_down")
        else:
            h32, route = _modulate_router_call(x3, mvec(i, 3), mvec(i, 4), moe_w_router[j])
            tme = _tile(m, 512, tg)
            src_rows, row_gate, tile_expert, tile_valid, pos1, pos2 = _route(route.reshape(m, LANES), n_experts, tme)
            hs = _gather_cast_call(h32.reshape(m, d), src_rows, tme)
            act = _swiglu_call(hs, moe_w_gate[j], moe_w_up[j], tile_expert, tile_valid, tme, _tile(d_ff, 512), "moe_up")
            ys = _matmul_rowscale(act, moe_w_down[j], row_gate, tile_expert, tile_valid, tme,
                                  _tile(d, 2048), _tile(d_ff, 1024), "moe_down")
            x3 = _combine_call(ys, pos1, pos2, x3, mvec(i, 5), _tile(m, 256, tg))

    x2 = x3.reshape(m, d)
    y_prompt = _final_norm_call(x2, final_norm_w, 0, mp, "final_norm_prompt").reshape(bp, lp, d)
    y_sample = _final_norm_call(x2, final_norm_w, mp, msamp, "final_norm_sample").reshape(bs, ls, d)
    return (y_prompt, y_sample, jnp.stack(ssm_p), jnp.stack(conv_p), jnp.stack(k_p), jnp.stack(v_p),
            jnp.stack(ssm_s), jnp.stack(conv_s), jnp.stack(k_s), jnp.stack(v_s))
```

```python
import functools
import math

import numpy as np
import jax
import jax.numpy as jnp
from jax import lax
from jax.experimental import pallas as pl
from jax.experimental.pallas import tpu as pltpu

F32 = jnp.float32
BF16 = jnp.bfloat16
EPS = 1e-6
SSD_CHUNK = 64
SB_BLOCK = 128
TOP_K = 2
LANES = 128
SUBLANES = 8
VMEM_LIMIT = 56 * 1024 * 1024


def _tile(n, pref, align=LANES):
    t = min(pref, n)
    t -= t % align
    while t >= align:
        if n % t == 0:
            return t
        t -= align
    return n


def _params(*sem):
    return pltpu.CompilerParams(dimension_semantics=sem, vmem_limit_bytes=VMEM_LIMIT)


def _sigmoid(x):
    return 1.0 / (1.0 + jnp.exp(-x))


def _silu(x):
    return x * _sigmoid(x)


def _softplus(x):
    return jnp.maximum(x, 0.0) + jnp.log1p(jnp.exp(-jnp.abs(x)))


def _split_bf16(x, parts):
    out = []
    r = x
    for _ in range(parts - 1):
        p = r.astype(BF16)
        out.append(p)
        r = r - p.astype(F32)
    out.append(r.astype(BF16))
    return out


def _dot(a, b):
    return jnp.dot(a, b, preferred_element_type=F32)


def _dot_nt(a, b):
    return lax.dot_general(a, b, (((1,), (1,)), ((), ())), preferred_element_type=F32)


def _sel_dot(x, onehot, parts=3):
    acc = None
    for p in _split_bf16(x, parts):
        t = _dot(p, onehot)
        acc = t if acc is None else acc + t
    return acc


def _sel_dot_left(onehot, x, parts=3):
    acc = None
    for p in _split_bf16(x, parts):
        t = _dot(onehot, p)
        acc = t if acc is None else acc + t
    return acc


def _mod_kernel(c_ref, w_ref, b_ref, o_ref):
    cs = _silu(c_ref[...]).astype(BF16)
    o_ref[0] = _dot(cs, w_ref[0].astype(BF16)) + b_ref[0]


def _mod_call(c_pad, w_mod, b_mod):
    depth, d, n = w_mod.shape
    rows = c_pad.shape[0]
    tn = _tile(n, 512)
    return pl.pallas_call(
        _mod_kernel,
        grid=(depth, n // tn),
        in_specs=[pl.BlockSpec((rows, d), lambda i, j: (0, 0)),
                  pl.BlockSpec((1, d, tn), lambda i, j: (i, 0, j)),
                  pl.BlockSpec((1, 1, tn), lambda i, j: (i, 0, j))],
        out_specs=pl.BlockSpec((1, rows, tn), lambda i, j: (i, 0, j)),
        out_shape=jax.ShapeDtypeStruct((depth, rows, n), F32),
        compiler_params=_params("arbitrary", "arbitrary"),
        name="adaln_mod",
    )(c_pad, w_mod, b_mod.reshape(depth, 1, n))


def _modulate_kernel(x_ref, sh_ref, sc_ref, o_ref):
    x = x_ref[...]
    ms = jnp.mean(x * x, axis=-1, keepdims=True)
    h = x * lax.rsqrt(ms + EPS) * (1.0 + sc_ref[...]) + sh_ref[...]
    o_ref[...] = h.astype(o_ref.dtype)


def _modulate_call(x3, sh, sc, out_dtype):
    g, tg, d = x3.shape
    gb = _tile(g, max(1, 256 // tg), 1)
    return pl.pallas_call(
        _modulate_kernel,
        grid=(g // gb,),
        in_specs=[pl.BlockSpec((gb, tg, d), lambda i: (i, 0, 0)),
                  pl.BlockSpec((gb, 1, d), lambda i: (i, 0, 0)),
                  pl.BlockSpec((gb, 1, d), lambda i: (i, 0, 0))],
        out_specs=pl.BlockSpec((gb, tg, d), lambda i: (i, 0, 0)),
        out_shape=jax.ShapeDtypeStruct((g, tg, d), out_dtype),
        compiler_params=_params("arbitrary"),
        name="adaln_modulate",
    )(x3, sh, sc)


def _modulate_router_kernel(x_ref, sh_ref, sc_ref, wr_ref, o_ref, r_ref, *, n_experts):
    x = x_ref[...]
    gb, tg, d = x.shape
    ms = jnp.mean(x * x, axis=-1, keepdims=True)
    h = x * lax.rsqrt(ms + EPS) * (1.0 + sc_ref[...]) + sh_ref[...]
    o_ref[...] = h
    h2 = h.reshape(gb * tg, d)
    h_hi, h_lo = _split_bf16(h2, 2)
    w_hi, w_lo = _split_bf16(wr_ref[...], 2)
    logits = _dot(h_hi, w_hi) + _dot(h_hi, w_lo) + _dot(h_lo, w_hi)
    lane = lax.broadcasted_iota(jnp.int32, logits.shape, 1).astype(F32)
    neg = jnp.float32(-jnp.inf)
    lg = jnp.where(lane < n_experts, logits, neg)
    m1 = jnp.max(lg, axis=1, keepdims=True)
    i1 = jnp.min(jnp.where(lg == m1, lane, float(LANES)), axis=1, keepdims=True)
    lg2 = jnp.where(lane == i1, neg, lg)
    m2 = jnp.max(lg2, axis=1, keepdims=True)
    i2 = jnp.min(jnp.where(lg2 == m2, lane, float(LANES)), axis=1, keepdims=True)
    e2 = jnp.exp(m2 - m1)
    g1 = 1.0 / (1.0 + e2)
    g2 = e2 / (1.0 + e2)
    out = jnp.where(lane == 0.0, i1,
                    jnp.where(lane == 1.0, i2,
                              jnp.where(lane == 2.0, g1, jnp.where(lane == 3.0, g2, 0.0))))
    r_ref[...] = out.reshape(gb, tg, LANES)


def _modulate_router_call(x3, sh, sc, w_router):
    g, tg, d = x3.shape
    n_experts = w_router.shape[1]
    wr = jnp.zeros((d, LANES), F32).at[:, :n_experts].set(w_router)
    gb = _tile(g, max(1, 256 // tg), 1)
    return pl.pallas_call(
        functools.partial(_modulate_router_kernel, n_experts=n_experts),
        grid=(g // gb,),
        in_specs=[pl.BlockSpec((gb, tg, d), lambda i: (i, 0, 0)),
                  pl.BlockSpec((gb, 1, d), lambda i: (i, 0, 0)),
                  pl.BlockSpec((gb, 1, d), lambda i: (i, 0, 0)),
                  pl.BlockSpec((d, LANES), lambda i: (0, 0))],
        out_specs=[pl.BlockSpec((gb, tg, d), lambda i: (i, 0, 0)),
                   pl.BlockSpec((gb, tg, LANES), lambda i: (i, 0, 0))],
        out_shape=[jax.ShapeDtypeStruct((g, tg, d), F32),
                   jax.ShapeDtypeStruct((g, tg, LANES), F32)],
        compiler_params=_params("arbitrary"),
        name="adaln_modulate_router",
    )(x3, sh, sc, wr)


def _ws_kernel(a_ref, w_ref, *rest):
    o_refs, wb_ref = rest[:-1], rest[-1]

    @pl.when(pl.program_id(1) == 0)
    def _():
        wb_ref[...] = w_ref[...].astype(BF16)

    acc = _dot(a_ref[...], wb_ref[...])
    for o_ref in o_refs:
        o_ref[...] = acc.astype(o_ref.dtype)


def _matmul_ws(a, w, col_off, n_cols, tn, tm, name, out_dtypes=(F32,)):
    m, k = a.shape
    off = col_off // tn
    assert col_off % tn == 0 and n_cols % tn == 0 and m % tm == 0
    return pl.pallas_call(
        _ws_kernel,
        grid=(n_cols // tn, m // tm),
        in_specs=[pl.BlockSpec((tm, k), lambda n, i: (i, 0)),
                  pl.BlockSpec((k, tn), lambda n, i: (0, n + off))],
        out_specs=[pl.BlockSpec((tm, tn), lambda n, i: (i, n)) for _ in out_dtypes],
        out_shape=[jax.ShapeDtypeStruct((m, n_cols), dt) for dt in out_dtypes],
        scratch_shapes=[pltpu.VMEM((k, tn), BF16)],
        compiler_params=_params("arbitrary", "arbitrary"),
        name=name,
    )(a, w)


def _swiglu_kernel(te_ref, tv_ref, a_ref, wg_ref, wu_ref, o_ref, wgb_ref, wub_ref):
    i = pl.program_id(1)
    fresh = jnp.logical_or(i == 0, te_ref[i] != te_ref[jnp.maximum(i - 1, 0)])

    @pl.when(fresh)
    def _():
        wgb_ref[...] = wg_ref[0].astype(BF16)
        wub_ref[...] = wu_ref[0].astype(BF16)

    @pl.when(tv_ref[i] != 0)
    def _():
        a = a_ref[...]
        g = _dot(a, wgb_ref[...])
        u = _dot(a, wub_ref[...])
        o_ref[...] = (_silu(g) * u).astype(o_ref.dtype)

    @pl.when(tv_ref[i] == 0)
    def _():
        o_ref[...] = jnp.zeros_like(o_ref)


def _swiglu_call(a, w_gate, w_up, tile_expert, tile_valid, tm, tf, name):
    m, k = a.shape
    _, _, f = w_gate.shape
    nt = m // tm
    grid_spec = pltpu.PrefetchScalarGridSpec(
        num_scalar_prefetch=2,
        grid=(f // tf, nt),
        in_specs=[pl.BlockSpec((tm, k), lambda j, i, te, tv: (i, 0)),
                  pl.BlockSpec((1, k, tf), lambda j, i, te, tv: (te[i], 0, j)),
                  pl.BlockSpec((1, k, tf), lambda j, i, te, tv: (te[i], 0, j))],
        out_specs=pl.BlockSpec((tm, tf), lambda j, i, te, tv: (i, j)),
        scratch_shapes=[pltpu.VMEM((k, tf), BF16), pltpu.VMEM((k, tf), BF16)],
    )
    return pl.pallas_call(
        _swiglu_kernel,
        grid_spec=grid_spec,
        out_shape=jax.ShapeDtypeStruct((m, f), BF16),
        compiler_params=_params("arbitrary", "arbitrary"),
        name=name,
    )(tile_expert, tile_valid, a, w_gate, w_up)


def _expert_up_kernel(start_ref, count_ref, hs_ref, wg_ref, wu_ref, act_ref,
                      wgb_ref, wub_ref, a_buf, o_buf, zero_buf, a_sem, o_sem, z_sem, *, tr, tf, n_tiles_total):
    j, e = pl.program_id(0), pl.program_id(1)
    n = count_ref[e]
    row0 = start_ref[e]
    col0 = pl.multiple_of(j * tf, tf)
    wgb_ref[...] = wg_ref[0].astype(BF16)
    wub_ref[...] = wu_ref[0].astype(BF16)

    def rows(r):
        return pl.ds(pl.multiple_of(row0 + r * tr, tr), tr)

    def a_copy(r, slot):
        return pltpu.make_async_copy(hs_ref.at[rows(r), :], a_buf.at[slot], a_sem.at[slot])

    def o_copy(r, slot):
        return pltpu.make_async_copy(o_buf.at[slot], act_ref.at[rows(r), pl.ds(col0, tf)], o_sem.at[slot])

    @pl.when(n > 0)
    def _():
        a_copy(0, 0).start()

    def body(r, carry):
        slot = lax.rem(r, 2)
        a_copy(r, slot).wait()

        @pl.when(r + 1 < n)
        def _():
            a_copy(r + 1, 1 - slot).start()

        a = a_buf[slot]
        g = _dot(a, wgb_ref[...])
        u = _dot(a, wub_ref[...])

        @pl.when(r >= 2)
        def _():
            o_copy(r - 2, slot).wait()

        o_buf[slot] = (_silu(g) * u).astype(o_buf.dtype)
        o_copy(r, slot).start()
        return carry

    lax.fori_loop(0, n, body, 0)

    @pl.when(n >= 2)
    def _():
        o_copy(n - 2, lax.rem(n, 2)).wait()

    @pl.when(n >= 1)
    def _():
        o_copy(n - 1, lax.rem(n + 1, 2)).wait()

    @pl.when(e == pl.num_programs(1) - 1)
    def _():
        first = (row0 + n * tr) // tr
        zero_buf[...] = jnp.zeros_like(zero_buf)

        def z_copy(t):
            dst = act_ref.at[pl.ds(pl.multiple_of(t * tr, tr), tr), pl.ds(col0, tf)]
            return pltpu.make_async_copy(zero_buf, dst, z_sem)

        def z_start(t, carry):
            z_copy(t).start()
            return carry

        def z_wait(t, carry):
            z_copy(t).wait()
            return carry

        lax.fori_loop(first, n_tiles_total, z_start, 0)
        lax.fori_loop(first, n_tiles_total, z_wait, 0)


def _expert_up_call(hs, w_gate, w_up, group_start, group_tiles, tr, tf, name):
    m, k = hs.shape
    n_experts, _, f = w_gate.shape
    grid_spec = pltpu.PrefetchScalarGridSpec(
        num_scalar_prefetch=2,
        grid=(f // tf, n_experts),
        in_specs=[pl.BlockSpec(memory_space=pl.ANY),
                  pl.BlockSpec((1, k, tf), lambda j, e, st, ct: (e, 0, j)),
                  pl.BlockSpec((1, k, tf), lambda j, e, st, ct: (e, 0, j))],
        out_specs=pl.BlockSpec(memory_space=pl.ANY),
        scratch_shapes=[pltpu.VMEM((k, tf), BF16), pltpu.VMEM((k, tf), BF16),
                        pltpu.VMEM((2, tr, k), BF16), pltpu.VMEM((2, tr, tf), BF16), pltpu.VMEM((tr, tf), BF16),
                        pltpu.SemaphoreType.DMA((2,)), pltpu.SemaphoreType.DMA((2,)), pltpu.SemaphoreType.DMA(())],
    )
    return pl.pallas_call(
        functools.partial(_expert_up_kernel, tr=tr, tf=tf, n_tiles_total=m // tr),
        grid_spec=grid_spec,
        out_shape=jax.ShapeDtypeStruct((m, f), BF16),
        compiler_params=_params("arbitrary", "arbitrary"),
        name=name,
    )(group_start, group_tiles, hs, w_gate, w_up)


def _kt_accumulate(tv_ref, a_ref, w_ref, acc_ref):
    k = pl.program_id(2)

    @pl.when(k == 0)
    def _():
        acc_ref[...] = jnp.zeros_like(acc_ref)

    @pl.when(tv_ref[pl.program_id(0)] != 0)
    def _():
        acc_ref[...] += _dot(a_ref[...], w_ref[0].astype(BF16))


def _kt_residual_kernel(te_ref, tv_ref, a_ref, w_ref, x_ref, g_ref, o_ref, acc_ref):
    _kt_accumulate(tv_ref, a_ref, w_ref, acc_ref)

    @pl.when(pl.program_id(2) == pl.num_programs(2) - 1)
    def _():
        o_ref[...] = x_ref[...] + g_ref[...] * acc_ref[...].reshape(o_ref.shape)


def _expert_down_kernel(iblk_ref, iexp_ref, ifirst_ref, ilast_ref, ivalid_ref, sube_ref, subv_ref,
                        a_ref, w_ref, rs_ref, o_ref, acc_ref, *, nsub, tr):
    it, k = pl.program_id(1), pl.program_id(2)
    blk = iblk_ref[it]
    e = iexp_ref[it]
    live = ivalid_ref[it] != 0

    @pl.when(jnp.logical_and(k == 0, ifirst_ref[it] != 0))
    def _():
        acc_ref[...] = jnp.zeros_like(acc_ref)

    mine = [jnp.logical_and(subv_ref[blk * nsub + s] != 0, sube_ref[blk * nsub + s] == e) for s in range(nsub)]
    whole = functools.reduce(jnp.logical_and, mine)

    @pl.when(jnp.logical_and(live, whole))
    def _():
        acc_ref[...] += _dot(a_ref[...], w_ref[0].astype(BF16))

    for s in range(nsub):
        @pl.when(jnp.logical_and(live, jnp.logical_and(jnp.logical_not(whole), mine[s])))
        def _(s=s):
            acc_ref[s * tr:(s + 1) * tr, :] += _dot(a_ref[s * tr:(s + 1) * tr, :], w_ref[0].astype(BF16))

    @pl.when(jnp.logical_and(k == pl.num_programs(2) - 1, ilast_ref[it] != 0))
    def _():
        o_ref[...] = rs_ref[0] * acc_ref[...]


def _matmul_residual(a, w, x3, gate3, tm, tn, tk, name):
    m, k = a.shape
    n = w.shape[-1]
    g, tg, _ = x3.shape
    gb = tm // tg
    nt = m // tm
    te = jnp.zeros((nt,), jnp.int32)
    tv = jnp.ones((nt,), jnp.int32)
    grid_spec = pltpu.PrefetchScalarGridSpec(
        num_scalar_prefetch=2,
        grid=(nt, n // tn, k // tk),
        in_specs=[pl.BlockSpec((tm, tk), lambda i, j, kk, te, tv: (i, kk)),
                  pl.BlockSpec((1, tk, tn), lambda i, j, kk, te, tv: (te[i], kk, j)),
                  pl.BlockSpec((gb, tg, tn), lambda i, j, kk, te, tv: (i, 0, j)),
                  pl.BlockSpec((gb, 1, tn), lambda i, j, kk, te, tv: (i, 0, j))],
        out_specs=pl.BlockSpec((gb, tg, tn), lambda i, j, kk, te, tv: (i, 0, j)),
        scratch_shapes=[pltpu.VMEM((tm, tn), F32)],
    )
    return pl.pallas_call(
        _kt_residual_kernel,
        grid_spec=grid_spec,
        out_shape=jax.ShapeDtypeStruct(x3.shape, F32),
        compiler_params=_params("arbitrary", "arbitrary", "arbitrary"),
        name=name,
    )(te, tv, a, w.reshape((1,) + w.shape[-2:]), x3, gate3)


def _expert_down_call(a, w, rowscale, items, tr, nsub, tn, tk, name):
    m, k = a.shape
    n = w.shape[-1]
    tb = tr * nsub
    nk = k // tk
    n_items = items[0].shape[0]

    def kk_of(it, kk, pf):
        return jnp.where(pf[4][it] != 0, kk, nk - 1)

    grid_spec = pltpu.PrefetchScalarGridSpec(
        num_scalar_prefetch=7,
        grid=(n // tn, n_items, nk),
        in_specs=[pl.BlockSpec((tb, tk), lambda j, it, kk, *pf: (pf[0][it], kk_of(it, kk, pf))),
                  pl.BlockSpec((1, tk, tn), lambda j, it, kk, *pf: (pf[1][it], kk_of(it, kk, pf), j)),
                  pl.BlockSpec((1, tb, 1), lambda j, it, kk, *pf: (pf[0][it], 0, 0))],
        out_specs=pl.BlockSpec((tb, tn), lambda j, it, kk, *pf: (pf[0][it], j)),
        scratch_shapes=[pltpu.VMEM((tb, tn), F32)],
    )
    return pl.pallas_call(
        functools.partial(_expert_down_kernel, nsub=nsub, tr=tr),
        grid_spec=grid_spec,
        out_shape=jax.ShapeDtypeStruct((m, n), F32),
        compiler_params=_params("arbitrary", "arbitrary", "arbitrary"),
        name=name,
    )(*items, a, w, rowscale.reshape(m // tb, tb, 1))


def _conv_kernel(x_ref, prev_ref, w_ref, b_ref, o_ref, xp_ref, *, tl, taps):
    t = pl.program_id(2)
    hist = taps - 1
    top = SUBLANES - hist

    @pl.when(t == 0)
    def _():
        xp_ref[top:SUBLANES, :] = prev_ref[0]

    @pl.when(t > 0)
    def _():
        xp_ref[0:SUBLANES, :] = xp_ref[tl:tl + SUBLANES, :]

    x = x_ref[...]
    xp_ref[SUBLANES:SUBLANES + tl, :] = x
    w = w_ref[...]
    acc = b_ref[...] + w[hist:taps] * x
    for k in range(hist):
        acc = acc + w[k:k + 1] * xp_ref[top + k:top + k + tl, :]
    o_ref[...] = _silu(acc)


def _conv_call(proj, row0, nb, seq, col0, conv_prev, conv_w, conv_b, name):
    taps, c = conv_w.shape
    tl = _tile(seq, 512, SUBLANES)
    tc = _tile(math.gcd(c, col0) if col0 else c, 1024)
    nt = seq // tl
    assert row0 % tl == 0 and col0 % tc == 0
    rb0, cb0 = row0 // tl, col0 // tc
    return pl.pallas_call(
        functools.partial(_conv_kernel, tl=tl, taps=taps),
        grid=(nb, c // tc, nt),
        in_specs=[pl.BlockSpec((tl, tc), lambda b, j, t: (rb0 + b * nt + t, cb0 + j)),
                  pl.BlockSpec((1, taps - 1, tc), lambda b, j, t: (b, 0, j)),
                  pl.BlockSpec((taps, tc), lambda b, j, t: (0, j)),
                  pl.BlockSpec((1, tc), lambda b, j, t: (0, j))],
        out_specs=pl.BlockSpec((tl, tc), lambda b, j, t: (b * nt + t, j)),
        out_shape=jax.ShapeDtypeStruct((nb * seq, c), F32),
        scratch_shapes=[pltpu.VMEM((tl + SUBLANES, tc), F32)],
        compiler_params=_params("arbitrary", "arbitrary", "arbitrary"),
        name=name,
    )(proj, conv_prev, conv_w, conv_b.reshape(1, c))


def _ssd_kernel(x_ref, b_ref, c_ref, dtr_ref, z_ref, sel_ref, e_ref, bias_ref, alog_ref, dskip_ref, nw_ref,
                s0_ref, g_ref, s_ref, st_ref, *, q, heads, hdim):
    c = pl.program_id(2)
    rp = heads * hdim
    per_lane = LANES // hdim

    @pl.when(c == 0)
    def _():
        st_ref[...] = s0_ref[0, 0].T

    dt_all = _softplus(dtr_ref[...] + bias_ref[...])
    a_all = -jnp.exp(alog_ref[...])
    sel = sel_ref[0]
    dtg = _sel_dot(dt_all, sel)
    dag = _sel_dot(dt_all * a_all, sel)
    ri = lax.broadcasted_iota(jnp.int32, (q, q), 0)
    ci = lax.broadcasted_iota(jnp.int32, (q, q), 1)
    causal = ri >= ci
    tril = jnp.where(causal, 1.0, 0.0).astype(BF16)
    acs = _sel_dot_left(tril, dag)
    acs_t = acs.T
    expand = e_ref[...]
    dt_exp = _sel_dot(dtg, expand)
    acs_exp = _sel_dot(acs, expand)

    x = x_ref[...]
    xdt = x * dt_exp
    bm = b_ref[...].astype(BF16)
    cm = c_ref[...].astype(BF16)
    cb = _dot_nt(cm, bm)

    st = st_ref[...]
    y = _dot(cm, st.astype(BF16)) * jnp.exp(acs_exp)
    last = acs_exp[q - 1:q, :]
    xs = (xdt * jnp.exp(last - acs_exp)).astype(BF16)
    bt = b_ref[...].T.astype(BF16)
    st_ref[...] = st * jnp.exp(last) + _dot(bt, xs)

    lane = lax.broadcasted_iota(jnp.int32, (q, LANES), 1)
    xdt_b = xdt
    pieces = []
    for pr in range(heads // per_lane):
        xp = xdt_b[:, pr * LANES:(pr + 1) * LANES]
        acc = None
        for hh in range(per_lane):
            r = pr * per_lane + hh
            seg = acs[:, r:r + 1] - acs_t[r:r + 1, :]
            dec = jnp.exp(jnp.where(causal, seg, -jnp.inf))
            m = (cb * dec).astype(BF16)
            in_head = jnp.logical_and(lane >= hh * hdim, lane < (hh + 1) * hdim)
            xm = jnp.where(in_head, xp, 0.0).astype(BF16)
            t = _dot(m, xm)
            acc = t if acc is None else acc + t
        pieces.append(acc)
    y = y + jnp.concatenate(pieces, axis=1) + dskip_ref[...] * x

    gt = y * _silu(z_ref[...])
    ms = jnp.mean(gt * gt, axis=-1, keepdims=True)
    g_ref[...] = (gt * lax.rsqrt(ms + EPS) * nw_ref[...]).astype(g_ref.dtype)

    @pl.when(c == pl.num_programs(2) - 1)
    def _():
        s_ref[0, 0] = st_ref[...].T


def _ssd_call(xbc, proj, dt_raw, row0, nb, seq, s0, consts, name):
    sel, expand, bias, alog, dskip, nw = consts
    groups, n_heads, _ = sel.shape
    _, _, rp, n = s0.shape
    heads = n_heads // groups
    hdim = rp // heads
    inner = groups * rp
    q = SSD_CHUNK if seq % SSD_CHUNK == 0 else seq
    nc = seq // q
    assert row0 % q == 0 and LANES % hdim == 0 and heads % (LANES // hdim) == 0 and heads <= LANES
    rb0 = row0 // q
    nb_cols = inner // n
    return pl.pallas_call(
        functools.partial(_ssd_kernel, q=q, heads=heads, hdim=hdim),
        grid=(nb, groups, nc),
        in_specs=[pl.BlockSpec((q, rp), lambda b, g, c: (b * nc + c, g)),
                  pl.BlockSpec((q, n), lambda b, g, c: (b * nc + c, nb_cols + g)),
                  pl.BlockSpec((q, n), lambda b, g, c: (b * nc + c, nb_cols + groups + g)),
                  pl.BlockSpec((q, n_heads), lambda b, g, c: (rb0 + b * nc + c, 0)),
                  pl.BlockSpec((q, rp), lambda b, g, c: (rb0 + b * nc + c, g)),
                  pl.BlockSpec((1, n_heads, LANES), lambda b, g, c: (g, 0, 0)),
                  pl.BlockSpec((LANES, rp), lambda b, g, c: (0, 0)),
                  pl.BlockSpec((1, n_heads), lambda b, g, c: (0, 0)),
                  pl.BlockSpec((1, n_heads), lambda b, g, c: (0, 0)),
                  pl.BlockSpec((1, rp), lambda b, g, c: (0, g)),
                  pl.BlockSpec((1, rp), lambda b, g, c: (0, g)),
                  pl.BlockSpec((1, 1, rp, n), lambda b, g, c: (b, g, 0, 0))],
        out_specs=[pl.BlockSpec((q, rp), lambda b, g, c: (b * nc + c, g)),
                   pl.BlockSpec((1, 1, rp, n), lambda b, g, c: (b, g, 0, 0))],
        out_shape=[jax.ShapeDtypeStruct((nb * seq, inner), BF16),
                   jax.ShapeDtypeStruct(s0.shape, F32)],
        scratch_shapes=[pltpu.VMEM((n, rp), F32)],
        compiler_params=_params("arbitrary", "arbitrary", "arbitrary"),
        name=name,
    )(xbc, xbc, xbc, dt_raw, proj, sel, expand, bias, alog, dskip, nw, s0)


def _sb_weights(z, carry, suffix, strict_mask):
    sub = suffix.shape[0]
    tail = jnp.log(1.0 + jnp.exp(-jnp.abs(z)))
    log_beta = jnp.minimum(z, 0.0) - tail
    log_keep = jnp.minimum(-z, 0.0) - tail
    if strict_mask is not None:
        log_keep = jnp.where(strict_mask, log_keep, 0.0)
    nsub = z.shape[1] // sub
    later = [None] * nsub
    run = carry
    for j in reversed(range(nsub)):
        lk = log_keep[:, j * sub:(j + 1) * sub]
        later[j] = _sel_dot(lk, suffix, parts=2) + run
        run = run + jnp.sum(lk, axis=1, keepdims=True)
    later = later[0] if nsub == 1 else jnp.concatenate(later, axis=1)
    w = jnp.exp(log_beta + later)
    if strict_mask is not None:
        w = jnp.where(strict_mask, w, 0.0)
    return w, run


def _suffix_matrix(n):
    ri = lax.broadcasted_iota(jnp.int32, (n, n), 0)
    ci = lax.broadcasted_iota(jnp.int32, (n, n), 1)
    return jnp.where(ri > ci, 1.0, 0.0).astype(BF16)


def _stacked_strict_mask(nh, t):
    ri = lax.broadcasted_iota(jnp.int32, (nh * t, t), 0)
    ci = lax.broadcasted_iota(jnp.int32, (nh * t, t), 1)
    return ci < jnp.bitwise_and(ri, t - 1)


def _sb_scores(q_ref, load_k, nh, hd, scale, h0=0):
    zs = [_dot_nt(q_ref[:, (h0 + h) * hd:(h0 + h + 1) * hd], load_k(h0 + h)) for h in range(nh)]
    return jnp.concatenate(zs, axis=0) * scale


def _sb_values(w, load_v, nh, t, h0=0):
    wb = w.astype(BF16)
    return jnp.concatenate([_dot(wb[h * t:(h + 1) * t], load_v(h0 + h)) for h in range(nh)], axis=0)


def _sb_prompt_kernel(q_ref, k_ref, v_ref, o_ref, acc_ref, carry_ref, *, blk, nh, hd, scale):
    qi = pl.program_id(2)
    suffix = _suffix_matrix(blk)

    def visit(kb, carry, mask):
        ks = pl.multiple_of(kb * blk, blk)
        z = _sb_scores(q_ref, lambda h: k_ref[pl.ds(ks, blk), h * hd:(h + 1) * hd], nh, hd, scale)
        w, run = _sb_weights(z, carry, suffix, mask)
        return _sb_values(w, lambda h: v_ref[pl.ds(ks, blk), h * hd:(h + 1) * hd], nh, blk), run

    out, run = visit(qi, jnp.zeros((nh * blk, 1), F32), _stacked_strict_mask(nh, blk))
    acc_ref[...] = out
    carry_ref[...] = run

    def body(i, c):
        out, run = visit(qi - 1 - i, carry_ref[...], None)
        acc_ref[...] += out
        carry_ref[...] = run
        return c

    lax.fori_loop(0, qi, body, 0)
    for h in range(nh):
        o_ref[:, h * hd:(h + 1) * hd] = acc_ref[h * blk:(h + 1) * blk, :].astype(o_ref.dtype)


def _heads_per_step(n_heads):
    return SUBLANES if n_heads % SUBLANES == 0 else n_heads


def _sb_prompt_call(q16, kv16, row0, nb, seq, n_heads, hd, scale, name):
    blk = SB_BLOCK
    nq = seq // blk
    nh = _heads_per_step(n_heads)
    ng = n_heads // nh
    assert row0 % seq == 0 and hd % LANES == 0 and blk & (blk - 1) == 0
    sb0 = row0 // seq
    rb0 = row0 // blk
    return pl.pallas_call(
        functools.partial(_sb_prompt_kernel, blk=blk, nh=nh, hd=hd, scale=scale),
        grid=(nb, ng, nq),
        in_specs=[pl.BlockSpec((blk, nh * hd), lambda b, g, i: (rb0 + b * nq + i, g)),
                  pl.BlockSpec((seq, nh * hd), lambda b, g, i: (sb0 + b, g)),
                  pl.BlockSpec((seq, nh * hd), lambda b, g, i: (sb0 + b, ng + g))],
        out_specs=pl.BlockSpec((blk, nh * hd), lambda b, g, i: (b * nq + i, g)),
        out_shape=jax.ShapeDtypeStruct((nb * seq, n_heads * hd), BF16),
        scratch_shapes=[pltpu.VMEM((nh * blk, hd), F32), pltpu.VMEM((nh * blk, 1), F32)],
        compiler_params=_params("arbitrary", "arbitrary", "arbitrary"),
        name=name,
    )(q16, kv16, kv16)


def _sb_decode_kernel(q_ref, kn_ref, vn_ref, kc_ref, vc_ref, o_ref, acc_ref, carry_ref, *, nh, hd, sub, scale):
    p = pl.program_id(2)
    t = q_ref.shape[0]

    @pl.when(p == 0)
    def _():
        z = _sb_scores(q_ref, lambda h: kn_ref[:, h * hd:(h + 1) * hd], nh, hd, scale)
        w, run = _sb_weights(z, jnp.zeros((nh * t, 1), F32), _suffix_matrix(t), _stacked_strict_mask(nh, t))
        acc_ref[...] = _sb_values(w, lambda h: vn_ref[:, h * hd:(h + 1) * hd], nh, t)
        carry_ref[...] = run

    pb = kc_ref.shape[0]
    k_rows = kc_ref.reshape(pb * nh, hd)
    v_rows = vc_ref.reshape(pb * nh, hd)
    z = _sb_scores(q_ref, lambda h: k_rows[pl.ds(h, pb, stride=nh), :].astype(BF16), nh, hd, scale)
    w, run = _sb_weights(z, carry_ref[...], _suffix_matrix(sub), None)
    acc_ref[...] += _sb_values(w, lambda h: v_rows[pl.ds(h, pb, stride=nh), :].astype(BF16), nh, t)
    carry_ref[...] = run

    @pl.when(p == pl.num_programs(2) - 1)
    def _():
        for h in range(nh):
            o_ref[:, h * hd:(h + 1) * hd] = acc_ref[h * t:(h + 1) * t, :].astype(o_ref.dtype)


def _sb_decode_call(q16, kv16, row0, nb, seq, n_heads, hd, cache_k, cache_v, scale, name):
    past = cache_k.shape[1]
    nh = _heads_per_step(n_heads)
    ng = n_heads // nh
    pb = _tile(past, 512)
    sub = _tile(pb, 256)
    npb = past // pb
    assert row0 % seq == 0 and seq & (seq - 1) == 0
    sb0 = row0 // seq
    return pl.pallas_call(
        functools.partial(_sb_decode_kernel, nh=nh, hd=hd, sub=sub, scale=scale),
        grid=(nb, ng, npb),
        in_specs=[pl.BlockSpec((seq, nh * hd), lambda b, g, p: (sb0 + b, g)),
                  pl.BlockSpec((seq, nh * hd), lambda b, g, p: (sb0 + b, g)),
                  pl.BlockSpec((seq, nh * hd), lambda b, g, p: (sb0 + b, ng + g)),
                  pl.BlockSpec((None, pb, nh, hd), lambda b, g, p: (b, npb - 1 - p, g, 0)),
                  pl.BlockSpec((None, pb, nh, hd), lambda b, g, p: (b, npb - 1 - p, g, 0))],
        out_specs=pl.BlockSpec((seq, nh * hd), lambda b, g, p: (b, g)),
        out_shape=jax.ShapeDtypeStruct((nb * seq, n_heads * hd), BF16),
        scratch_shapes=[pltpu.VMEM((nh * seq, hd), F32), pltpu.VMEM((nh * seq, 1), F32)],
        compiler_params=_params("arbitrary", "arbitrary", "arbitrary"),
        name=name,
    )(q16, kv16, kv16, cache_k, cache_v)


def _gather_cast_kernel(src_ref, h_ref, o_ref, buf_ref, sem):
    tm = buf_ref.shape[0]

    def row_copy(r):
        return pltpu.make_async_copy(h_ref.at[pl.ds(src_ref[0, 0, r], 1), :], buf_ref.at[pl.ds(r, 1), :], sem)

    def start(r, carry):
        row_copy(r).start()
        return carry

    def wait(r, carry):
        row_copy(r).wait()
        return carry

    lax.fori_loop(0, tm, start, 0)
    lax.fori_loop(0, tm, wait, 0)
    o_ref[...] = buf_ref[...].astype(o_ref.dtype)


def _gather_cast_call(h, src_rows, tm):
    m, d = h.shape
    nt = src_rows.shape[0] // tm
    return pl.pallas_call(
        _gather_cast_kernel,
        grid=(nt,),
        in_specs=[pl.BlockSpec((1, 1, tm), lambda i: (i, 0, 0), memory_space=pltpu.SMEM),
                  pl.BlockSpec(memory_space=pl.ANY)],
        out_specs=pl.BlockSpec((tm, d), lambda i: (i, 0)),
        out_shape=jax.ShapeDtypeStruct((nt * tm, d), BF16),
        scratch_shapes=[pltpu.VMEM((tm, d), F32), pltpu.SemaphoreType.DMA(())],
        compiler_params=_params("arbitrary"),
        name="moe_gather_rows",
    )(src_rows.reshape(nt, 1, tm), h)


def _combine_kernel(p1_ref, p2_ref, ys_ref, x_ref, g_ref, o_ref, buf_ref, sem):
    tm = buf_ref.shape[1]

    def copies(r):
        return (pltpu.make_async_copy(ys_ref.at[pl.ds(p1_ref[0, 0, r], 1), :], buf_ref.at[0, pl.ds(r, 1), :], sem),
                pltpu.make_async_copy(ys_ref.at[pl.ds(p2_ref[0, 0, r], 1), :], buf_ref.at[1, pl.ds(r, 1), :], sem))

    def start(r, carry):
        for cp in copies(r):
            cp.start()
        return carry

    def wait(r, carry):
        for cp in copies(r):
            cp.wait()
        return carry

    lax.fori_loop(0, tm, start, 0)
    lax.fori_loop(0, tm, wait, 0)
    f = (buf_ref[0] + buf_ref[1]).reshape(o_ref.shape)
    o_ref[...] = x_ref[...] + g_ref[...] * f


def _combine_call(ys, pos1, pos2, x3, gate3, tm):
    g, tg, d = x3.shape
    gb = tm // tg
    nt = (g * tg) // tm
    return pl.pallas_call(
        _combine_kernel,
        grid=(nt,),
        in_specs=[pl.BlockSpec((1, 1, tm), lambda i: (i, 0, 0), memory_space=pltpu.SMEM),
                  pl.BlockSpec((1, 1, tm), lambda i: (i, 0, 0), memory_space=pltpu.SMEM),
                  pl.BlockSpec(memory_space=pl.ANY),
                  pl.BlockSpec((gb, tg, d), lambda i: (i, 0, 0)),
                  pl.BlockSpec((gb, 1, d), lambda i: (i, 0, 0))],
        out_specs=pl.BlockSpec((gb, tg, d), lambda i: (i, 0, 0)),
        out_shape=jax.ShapeDtypeStruct(x3.shape, F32),
        scratch_shapes=[pltpu.VMEM((2, tm, d), F32), pltpu.SemaphoreType.DMA(())],
        compiler_params=_params("arbitrary"),
        name="moe_combine_rows",
    )(pos1.reshape(nt, 1, tm), pos2.reshape(nt, 1, tm), ys, x3, gate3)


def _final_norm_kernel(x_ref, w_ref, o_ref):
    x = x_ref[...]
    ms = jnp.mean(x * x, axis=-1, keepdims=True)
    o_ref[...] = x * lax.rsqrt(ms + EPS) * w_ref[...]


def _final_norm_call(x2, w, row0, n_rows, name):
    _, d = x2.shape
    tm = _tile(math.gcd(n_rows, row0) if row0 else n_rows, 256, SUBLANES)
    rb0 = row0 // tm
    return pl.pallas_call(
        _final_norm_kernel,
        grid=(n_rows // tm,),
        in_specs=[pl.BlockSpec((tm, d), lambda i: (rb0 + i, 0)), pl.BlockSpec((1, d), lambda i: (0, 0))],
        out_specs=pl.BlockSpec((tm, d), lambda i: (i, 0)),
        out_shape=jax.ShapeDtypeStruct((n_rows, d), F32),
        compiler_params=_params("arbitrary"),
        name=name,
    )(x2, w.reshape(1, d))


def _ssd_constants(groups, n_heads, hdim, dt_bias, a_log, d_skip, norm_w):
    heads = n_heads // groups
    sel = np.zeros((groups, n_heads, LANES), np.float32)
    for g in range(groups):
        for r in range(heads):
            sel[g, g * heads + r, r] = 1.0
    expand = np.zeros((LANES, heads * hdim), np.float32)
    for r in range(heads):
        expand[r, r * hdim:(r + 1) * hdim] = 1.0
    return (jnp.asarray(sel, BF16), jnp.asarray(expand, BF16),
            dt_bias.astype(F32).reshape(1, n_heads), a_log.astype(F32).reshape(1, n_heads),
            jnp.repeat(d_skip.astype(F32), hdim).reshape(1, n_heads * hdim), norm_w.reshape(1, n_heads * hdim))


def _route(route, n_experts, tr, nsub):
    m = route.shape[0]
    idx = route[:, :TOP_K].astype(jnp.int32)
    gates = route[:, TOP_K:2 * TOP_K]
    flat_e = idx.reshape(-1)
    n_assign = flat_e.shape[0]
    n_tiles = -(-(n_assign // tr + n_experts) // nsub) * nsub
    n_blocks = n_tiles // nsub
    order = jnp.argsort(flat_e, stable=True)
    counts = jnp.zeros((n_experts,), jnp.int32).at[flat_e].add(1)
    tiles = (counts + tr - 1) // tr
    padded = tiles * tr
    pad_end = jnp.cumsum(padded)
    pad_start = pad_end - padded
    cnt_start = jnp.cumsum(counts) - counts
    sorted_e = flat_e[order]
    dest_sorted = pad_start[sorted_e] + (jnp.arange(n_assign, dtype=jnp.int32) - cnt_start[sorted_e])
    dest = jnp.zeros((n_assign,), jnp.int32).at[order].set(dest_sorted)
    src_rows = jnp.zeros((n_tiles * tr,), jnp.int32).at[dest].set(jnp.arange(n_assign, dtype=jnp.int32) // TOP_K)
    row_gate = jnp.zeros((n_tiles * tr,), F32).at[dest].set(gates.reshape(-1))
    dest2 = dest.reshape(m, TOP_K)

    t_idx = jnp.arange(n_tiles, dtype=jnp.int32)
    t_start = t_idx * tr
    sub_e = jnp.minimum(jnp.searchsorted(pad_end, t_start, side="right"), n_experts - 1).astype(jnp.int32)
    sub_v = t_start < pad_end[-1]
    prev_e = jnp.concatenate([jnp.full((1,), -1, jnp.int32), sub_e[:-1]])
    run_start = jnp.logical_or(t_idx % nsub == 0, jnp.logical_and(sub_v, sub_e != prev_e))
    n_items = n_blocks + n_experts - 1
    item_sub = jnp.nonzero(run_start, size=n_items, fill_value=-1)[0].astype(jnp.int32)
    item_exists = item_sub >= 0
    n_exist = jnp.sum(item_exists.astype(jnp.int32))
    item_sub = jnp.where(item_exists, item_sub, item_sub[jnp.maximum(n_exist - 1, 0)])
    item_live = jnp.logical_and(item_exists, sub_v[item_sub])
    item_blk = item_sub // nsub
    item_exp = sub_e[item_sub]
    item_first = jnp.logical_and(item_exists, item_sub % nsub == 0)
    next_blk = jnp.concatenate([item_blk[1:], jnp.full((1,), -1, jnp.int32)])
    next_exists = jnp.concatenate([item_exists[1:], jnp.zeros((1,), bool)])
    item_last = jnp.logical_and(item_exists, jnp.logical_or(jnp.logical_not(next_exists), next_blk != item_blk))

    def i32(v):
        return v.astype(jnp.int32)

    items = (i32(item_blk), i32(item_exp), i32(item_first), i32(item_last), i32(item_live), sub_e, i32(sub_v))
    return src_rows, row_gate, i32(pad_start), i32(tiles), items, dest2[:, 0], dest2[:, 1]


def kernel(x_prompt, x_sample, state_ssm, state_conv, cache_k, cache_v, c_prompt, c_sample, w_mod, b_mod, ssd_w_in, ssd_conv_w, ssd_conv_b, ssd_dt_bias, ssd_a_log, ssd_d, ssd_norm_w, ssd_w_out, sb_w_qkv, sb_w_o, ffn_w_gate, ffn_w_up, ffn_w_down, moe_w_router, moe_w_gate, moe_w_up, moe_w_down, final_norm_w):
    bp, lp, d = x_prompt.shape
    bs, ls, _ = x_sample.shape
    depth = w_mod.shape[0]
    mp, msamp = bp * lp, bs * ls
    m = mp + msamp
    tg = math.gcd(lp, ls)
    assert tg % SUBLANES == 0
    n_groups = m // tg
    n_heads = ssd_a_log.shape[1]
    hdim, n_state = state_ssm.shape[3], state_ssm.shape[4]
    inner = n_heads * hdim
    conv_dim = ssd_conv_w.shape[2]
    ssd_groups = (conv_dim - inner) // (2 * n_state)
    sb_heads, sb_hd = cache_k.shape[3], cache_k.shape[4]
    n_experts = moe_w_router.shape[2]
    d_ff = ffn_w_gate.shape[2]
    tm = _tile(m, 1024, tg)

    seq_of_group = np.concatenate([np.repeat(np.arange(bp), lp // tg), bp + np.repeat(np.arange(bs), ls // tg)])
    n_seq = bp + bs
    rows = -(-n_seq // 16) * 16
    c_all = jnp.zeros((rows, d), F32).at[:n_seq].set(jnp.concatenate([c_prompt, c_sample], axis=0))
    mod = _mod_call(c_all, w_mod, b_mod)
    mod_g = mod[:, seq_of_group, :].reshape(depth, n_groups, 1, 6, d)

    def mvec(i, which):
        return mod_g[i, :, :, which, :]

    x3 = jnp.concatenate([x_prompt.reshape(mp, d), x_sample.reshape(msamp, d)], axis=0).reshape(n_groups, tg, d)
    zeros_ssm = jnp.zeros((bp,) + state_ssm.shape[2:], F32)
    zeros_conv = jnp.zeros((bp,) + state_conv.shape[2:], F32)
    ssm_p, conv_p, k_p, v_p, ssm_s, conv_s, k_s, v_s = [], [], [], [], [], [], [], []

    for i in range(depth):
        j = i // 2
        h = _modulate_call(x3, mvec(i, 0), mvec(i, 1), BF16).reshape(m, d)
        if i % 2 == 0:
            w_in = ssd_w_in[j]
            n_main = inner + conv_dim
            proj, = _matmul_ws(h, w_in, 0, n_main, _tile(n_main, 512), tm, "ssd_in_proj")
            dt_raw, = _matmul_ws(h, w_in, n_main, n_heads, n_heads, tm, "ssd_in_proj_dt")
            consts = _ssd_constants(ssd_groups, n_heads, hdim, ssd_dt_bias[j], ssd_a_log[j], ssd_d[j], ssd_norm_w[j])
            outs = []
            for (row0, nb, seq, prev, s0, conv_out, ssm_out, tag) in (
                    (0, bp, lp, zeros_conv, zeros_ssm, conv_p, ssm_p, "prompt"),
                    (mp, bs, ls, state_conv[j], state_ssm[j], conv_s, ssm_s, "sample")):
                xbc = _conv_call(proj, row0, nb, seq, inner, prev, ssd_conv_w[j], ssd_conv_b[j], "ssd_conv_" + tag)
                s0g = s0.astype(F32).reshape(nb, ssd_groups, (n_heads // ssd_groups) * hdim, n_state)
                g_out, s_fin = _ssd_call(xbc, proj, dt_raw, row0, nb, seq, s0g, consts, "ssd_scan_" + tag)
                outs.append(g_out)
                ssm_out.append(s_fin.reshape(nb, n_heads, hdim, n_state))
                hist = ssd_conv_w.shape[1] - 1
                last_rows = (row0 + np.arange(nb)[:, None] * seq + np.arange(seq - hist, seq)[None, :]).reshape(-1)
                conv_out.append(jnp.take(proj, last_rows, axis=0)[:, inner:].reshape(nb, hist, conv_dim))
            mix = jnp.concatenate(outs, axis=0)
            w_mix = ssd_w_out[j]
        else:
            hd_all = sb_heads * sb_hd
            tn = _tile(hd_all, 512)
            q16, = _matmul_ws(h, sb_w_qkv[j], 0, hd_all, tn, tm, "sb_q", (BF16,))
            kv32, kv16 = _matmul_ws(h, sb_w_qkv[j], hd_all, 2 * hd_all, tn, tm, "sb_kv", (F32, BF16))
            scale = float(sb_hd) ** -0.5
            o_p = _sb_prompt_call(q16, kv16, 0, bp, lp, sb_heads, sb_hd, scale, "sb_attn_prompt")
            o_s = _sb_decode_call(q16, kv16, mp, bs, ls, sb_heads, sb_hd, cache_k[j], cache_v[j], scale,
                                  "sb_attn_sample")
            mix = jnp.concatenate([o_p, o_s], axis=0)
            w_mix = sb_w_o[j]
            for (row0, nb, seq, k_out, v_out) in ((0, bp, lp, k_p, v_p), (mp, bs, ls, k_s, v_s)):
                blk = kv32[row0:row0 + nb * seq]
                k_out.append(blk[:, :hd_all].reshape(nb, seq, sb_heads, sb_hd))
                v_out.append(blk[:, hd_all:].reshape(nb, seq, sb_heads, sb_hd))
        kmix = mix.shape[1]
        x3 = _matmul_residual(mix, w_mix, x3, mvec(i, 2), tm, _tile(d, 1024), _tile(kmix, 1024), "mixer_out_proj")

        if i % 2 == 0:
            h = _modulate_call(x3, mvec(i, 3), mvec(i, 4), BF16).reshape(m, d)
            nt = m // tm
            act = _swiglu_call(h, ffn_w_gate[j][None], ffn_w_up[j][None], jnp.zeros((nt,), jnp.int32),
                               jnp.ones((nt,), jnp.int32), tm, _tile(d_ff, 256), "ffn_up")
            x3 = _matmul_residual(act, ffn_w_down[j], x3, mvec(i, 5), tm, _tile(d, 1024), _tile(d_ff, 1024), "ffn_down")
        else:
            h32, route = _modulate_router_call(x3, mvec(i, 3), mvec(i, 4), moe_w_router[j])
            tr = _tile(m, 256, tg)
            nsub = 4
            src_rows, row_gate, group_start, group_tiles, items, pos1, pos2 = _route(
                route.reshape(m, LANES), n_experts, tr, nsub)
            hs = _gather_cast_call(h32.reshape(m, d), src_rows, nsub * tr // 2)
            act = _expert_up_call(hs, moe_w_gate[j], moe_w_up[j], group_start, group_tiles, tr, _tile(d_ff, 512),
                                  "moe_up")
            ys = _expert_down_call(act, moe_w_down[j], row_gate, items, tr, nsub, _tile(d, 2048), _tile(d_ff, 1024),
                                   "moe_down")
            x3 = _combine_call(ys, pos1, pos2, x3, mvec(i, 5), _tile(m, 256, tg))

    x2 = x3.reshape(m, d)
    y_prompt = _final_norm_call(x2, final_norm_w, 0, mp, "final_norm_prompt").reshape(bp, lp, d)
    y_sample = _final_norm_call(x2, final_norm_w, mp, msamp, "final_norm_sample").reshape(bs, ls, d)
    return (y_prompt, y_sample, jnp.stack(ssm_p), jnp.stack(conv_p), jnp.stack(k_p), jnp.stack(v_p),
            jnp.stack(ssm_s), jnp.stack(conv_s), jnp.stack(k_s), jnp.stack(v_s))
```

```python
import functools
import math

import numpy as np
import jax
import jax.numpy as jnp
from jax import lax
from jax.experimental import pallas as pl
from jax.experimental.pallas import tpu as pltpu

F32 = jnp.float32
BF16 = jnp.bfloat16
EPS = 1e-6
SSD_CHUNK = 64
SB_BLOCK = 128
TOP_K = 2
LANES = 128
SUBLANES = 8
VMEM_LIMIT = 56 * 1024 * 1024
ROW_DMA_THREAD = 1


def _tile(n, pref, align=LANES):
    t = min(pref, n)
    t -= t % align
    while t >= align:
        if n % t == 0:
            return t
        t -= align
    return n


def _params(*sem):
    return pltpu.CompilerParams(dimension_semantics=sem, vmem_limit_bytes=VMEM_LIMIT)


def _sigmoid(x):
    return 1.0 / (1.0 + jnp.exp(-x))


def _silu(x):
    return x * _sigmoid(x)


def _softplus(x):
    return jnp.maximum(x, 0.0) + jnp.log1p(jnp.exp(-jnp.abs(x)))


def _split_bf16(x, parts):
    out = []
    r = x
    for _ in range(parts - 1):
        p = r.astype(BF16)
        out.append(p)
        r = r - p.astype(F32)
    out.append(r.astype(BF16))
    return out


def _dot(a, b):
    return jnp.dot(a, b, preferred_element_type=F32)


def _dot_nt(a, b):
    return lax.dot_general(a, b, (((1,), (1,)), ((), ())), preferred_element_type=F32)


def _sel_dot(x, onehot, parts=3):
    acc = None
    for p in _split_bf16(x, parts):
        t = _dot(p, onehot)
        acc = t if acc is None else acc + t
    return acc


def _sel_dot_left(onehot, x, parts=3):
    acc = None
    for p in _split_bf16(x, parts):
        t = _dot(onehot, p)
        acc = t if acc is None else acc + t
    return acc


def _mod_kernel(c_ref, w_ref, b_ref, o_ref):
    cs = _silu(c_ref[...]).astype(BF16)
    o_ref[0] = _dot(cs, w_ref[0].astype(BF16)) + b_ref[0]


def _mod_call(c_pad, w_mod, b_mod):
    depth, d, n = w_mod.shape
    rows = c_pad.shape[0]
    tn = _tile(n, 512)
    return pl.pallas_call(
        _mod_kernel,
        grid=(depth, n // tn),
        in_specs=[pl.BlockSpec((rows, d), lambda i, j: (0, 0)),
                  pl.BlockSpec((1, d, tn), lambda i, j: (i, 0, j)),
                  pl.BlockSpec((1, 1, tn), lambda i, j: (i, 0, j))],
        out_specs=pl.BlockSpec((1, rows, tn), lambda i, j: (i, 0, j)),
        out_shape=jax.ShapeDtypeStruct((depth, rows, n), F32),
        compiler_params=_params("arbitrary", "arbitrary"),
        name="adaln_mod",
    )(c_pad, w_mod, b_mod.reshape(depth, 1, n))


def _modulate_kernel(x_ref, sh_ref, sc_ref, o_ref):
    x = x_ref[...]
    ms = jnp.mean(x * x, axis=-1, keepdims=True)
    h = x * lax.rsqrt(ms + EPS) * (1.0 + sc_ref[...]) + sh_ref[...]
    o_ref[...] = h.astype(o_ref.dtype)


def _modulate_call(x3, sh, sc, out_dtype):
    g, tg, d = x3.shape
    gb = _tile(g, max(1, 256 // tg), 1)
    return pl.pallas_call(
        _modulate_kernel,
        grid=(g // gb,),
        in_specs=[pl.BlockSpec((gb, tg, d), lambda i: (i, 0, 0)),
                  pl.BlockSpec((gb, 1, d), lambda i: (i, 0, 0)),
                  pl.BlockSpec((gb, 1, d), lambda i: (i, 0, 0))],
        out_specs=pl.BlockSpec((gb, tg, d), lambda i: (i, 0, 0)),
        out_shape=jax.ShapeDtypeStruct((g, tg, d), out_dtype),
        compiler_params=_params("arbitrary"),
        name="adaln_modulate",
    )(x3, sh, sc)


def _modulate_router_kernel(x_ref, sh_ref, sc_ref, wr_ref, o_ref, r_ref, *, n_experts):
    x = x_ref[...]
    gb, tg, d = x.shape
    ms = jnp.mean(x * x, axis=-1, keepdims=True)
    h = x * lax.rsqrt(ms + EPS) * (1.0 + sc_ref[...]) + sh_ref[...]
    o_ref[...] = h
    h2 = h.reshape(gb * tg, d)
    h_hi, h_lo = _split_bf16(h2, 2)
    w_hi, w_lo = _split_bf16(wr_ref[...], 2)
    logits = _dot(h_hi, w_hi) + _dot(h_hi, w_lo) + _dot(h_lo, w_hi)
    lane = lax.broadcasted_iota(jnp.int32, logits.shape, 1).astype(F32)
    neg = jnp.float32(-jnp.inf)
    lg = jnp.where(lane < n_experts, logits, neg)
    m1 = jnp.max(lg, axis=1, keepdims=True)
    i1 = jnp.min(jnp.where(lg == m1, lane, float(LANES)), axis=1, keepdims=True)
    lg2 = jnp.where(lane == i1, neg, lg)
    m2 = jnp.max(lg2, axis=1, keepdims=True)
    i2 = jnp.min(jnp.where(lg2 == m2, lane, float(LANES)), axis=1, keepdims=True)
    e2 = jnp.exp(m2 - m1)
    g1 = 1.0 / (1.0 + e2)
    g2 = e2 / (1.0 + e2)
    out = jnp.where(lane == 0.0, i1,
                    jnp.where(lane == 1.0, i2,
                              jnp.where(lane == 2.0, g1, jnp.where(lane == 3.0, g2, 0.0))))
    r_ref[...] = out.reshape(gb, tg, LANES)


def _modulate_router_call(x3, sh, sc, w_router):
    g, tg, d = x3.shape
    n_experts = w_router.shape[1]
    wr = jnp.zeros((d, LANES), F32).at[:, :n_experts].set(w_router)
    gb = _tile(g, max(1, 256 // tg), 1)
    return pl.pallas_call(
        functools.partial(_modulate_router_kernel, n_experts=n_experts),
        grid=(g // gb,),
        in_specs=[pl.BlockSpec((gb, tg, d), lambda i: (i, 0, 0)),
                  pl.BlockSpec((gb, 1, d), lambda i: (i, 0, 0)),
                  pl.BlockSpec((gb, 1, d), lambda i: (i, 0, 0)),
                  pl.BlockSpec((d, LANES), lambda i: (0, 0))],
        out_specs=[pl.BlockSpec((gb, tg, d), lambda i: (i, 0, 0)),
                   pl.BlockSpec((gb, tg, LANES), lambda i: (i, 0, 0))],
        out_shape=[jax.ShapeDtypeStruct((g, tg, d), F32),
                   jax.ShapeDtypeStruct((g, tg, LANES), F32)],
        compiler_params=_params("arbitrary"),
        name="adaln_modulate_router",
    )(x3, sh, sc, wr)


def _ws_kernel(a_ref, w_ref, *rest):
    o_refs, wb_ref = rest[:-1], rest[-1]

    @pl.when(pl.program_id(1) == 0)
    def _():
        wb_ref[...] = w_ref[...].astype(BF16)

    acc = _dot(a_ref[...], wb_ref[...])
    for o_ref in o_refs:
        o_ref[...] = acc.astype(o_ref.dtype)


def _matmul_ws(a, w, col_off, n_cols, tn, tm, name, out_dtypes=(F32,)):
    m, k = a.shape
    off = col_off // tn
    assert col_off % tn == 0 and n_cols % tn == 0 and m % tm == 0
    return pl.pallas_call(
        _ws_kernel,
        grid=(n_cols // tn, m // tm),
        in_specs=[pl.BlockSpec((tm, k), lambda n, i: (i, 0)),
                  pl.BlockSpec((k, tn), lambda n, i: (0, n + off))],
        out_specs=[pl.BlockSpec((tm, tn), lambda n, i: (i, n)) for _ in out_dtypes],
        out_shape=[jax.ShapeDtypeStruct((m, n_cols), dt) for dt in out_dtypes],
        scratch_shapes=[pltpu.VMEM((k, tn), BF16)],
        compiler_params=_params("arbitrary", "arbitrary"),
        name=name,
    )(a, w)


def _swiglu_kernel(te_ref, tv_ref, a_ref, wg_ref, wu_ref, o_ref, wgb_ref, wub_ref):
    i = pl.program_id(1)
    fresh = jnp.logical_or(i == 0, te_ref[i] != te_ref[jnp.maximum(i - 1, 0)])

    @pl.when(fresh)
    def _():
        wgb_ref[...] = wg_ref[0].astype(BF16)
        wub_ref[...] = wu_ref[0].astype(BF16)

    @pl.when(tv_ref[i] != 0)
    def _():
        a = a_ref[...]
        g = _dot(a, wgb_ref[...])
        u = _dot(a, wub_ref[...])
        o_ref[...] = (_silu(g) * u).astype(o_ref.dtype)

    @pl.when(tv_ref[i] == 0)
    def _():
        o_ref[...] = jnp.zeros_like(o_ref)


def _swiglu_call(a, w_gate, w_up, tile_expert, tile_valid, tm, tf, name):
    m, k = a.shape
    _, _, f = w_gate.shape
    nt = m // tm
    grid_spec = pltpu.PrefetchScalarGridSpec(
        num_scalar_prefetch=2,
        grid=(f // tf, nt),
        in_specs=[pl.BlockSpec((tm, k), lambda j, i, te, tv: (i, 0)),
                  pl.BlockSpec((1, k, tf), lambda j, i, te, tv: (te[i], 0, j)),
                  pl.BlockSpec((1, k, tf), lambda j, i, te, tv: (te[i], 0, j))],
        out_specs=pl.BlockSpec((tm, tf), lambda j, i, te, tv: (i, j)),
        scratch_shapes=[pltpu.VMEM((k, tf), BF16), pltpu.VMEM((k, tf), BF16)],
    )
    return pl.pallas_call(
        _swiglu_kernel,
        grid_spec=grid_spec,
        out_shape=jax.ShapeDtypeStruct((m, f), BF16),
        compiler_params=_params("arbitrary", "arbitrary"),
        name=name,
    )(tile_expert, tile_valid, a, w_gate, w_up)


def _expert_up_kernel(start_ref, count_ref, hs_ref, wg_ref, wu_ref, act_ref,
                      wgb_ref, wub_ref, a_buf, o_buf, zero_buf, a_sem, o_sem, z_sem, *, tr, tf, n_tiles_total):
    j, e = pl.program_id(0), pl.program_id(1)
    n = count_ref[e]
    row0 = start_ref[e]
    col0 = pl.multiple_of(j * tf, tf)

    def rows(r):
        return pl.ds(pl.multiple_of(row0 + r * tr, tr), tr)

    def a_copy(r, slot):
        return pltpu.make_async_copy(hs_ref.at[rows(r), :], a_buf.at[slot], a_sem.at[slot])

    def o_copy(r, slot):
        return pltpu.make_async_copy(o_buf.at[slot], act_ref.at[rows(r), pl.ds(col0, tf)], o_sem.at[slot])

    @pl.when(n > 0)
    def _():
        a_copy(0, 0).start(priority=ROW_DMA_THREAD)

    wgb_ref[...] = wg_ref[0].astype(BF16)
    wub_ref[...] = wu_ref[0].astype(BF16)

    def body(r, carry):
        slot = lax.rem(r, 2)
        a_copy(r, slot).wait()

        @pl.when(r + 1 < n)
        def _():
            a_copy(r + 1, 1 - slot).start(priority=ROW_DMA_THREAD)

        a = a_buf[slot]
        g = _dot(a, wgb_ref[...])
        u = _dot(a, wub_ref[...])

        @pl.when(r >= 2)
        def _():
            o_copy(r - 2, slot).wait()

        o_buf[slot] = (_silu(g) * u).astype(o_buf.dtype)
        o_copy(r, slot).start(priority=ROW_DMA_THREAD)
        return carry

    lax.fori_loop(0, n, body, 0)

    @pl.when(n >= 2)
    def _():
        o_copy(n - 2, lax.rem(n, 2)).wait()

    @pl.when(n >= 1)
    def _():
        o_copy(n - 1, lax.rem(n + 1, 2)).wait()

    @pl.when(e == pl.num_programs(1) - 1)
    def _():
        first = (row0 + n * tr) // tr
        zero_buf[...] = jnp.zeros_like(zero_buf)

        def z_copy(t):
            dst = act_ref.at[pl.ds(pl.multiple_of(t * tr, tr), tr), pl.ds(col0, tf)]
            return pltpu.make_async_copy(zero_buf, dst, z_sem)

        def z_start(t, carry):
            z_copy(t).start()
            return carry

        def z_wait(t, carry):
            z_copy(t).wait()
            return carry

        lax.fori_loop(first, n_tiles_total, z_start, 0)
        lax.fori_loop(first, n_tiles_total, z_wait, 0)


def _expert_up_call(hs, w_gate, w_up, group_start, group_tiles, tr, tf, name):
    m, k = hs.shape
    n_experts, _, f = w_gate.shape
    grid_spec = pltpu.PrefetchScalarGridSpec(
        num_scalar_prefetch=2,
        grid=(f // tf, n_experts),
        in_specs=[pl.BlockSpec(memory_space=pl.ANY),
                  pl.BlockSpec((1, k, tf), lambda j, e, st, ct: (e, 0, j)),
                  pl.BlockSpec((1, k, tf), lambda j, e, st, ct: (e, 0, j))],
        out_specs=pl.BlockSpec(memory_space=pl.ANY),
        scratch_shapes=[pltpu.VMEM((k, tf), BF16), pltpu.VMEM((k, tf), BF16),
                        pltpu.VMEM((2, tr, k), BF16), pltpu.VMEM((2, tr, tf), BF16), pltpu.VMEM((tr, tf), BF16),
                        pltpu.SemaphoreType.DMA((2,)), pltpu.SemaphoreType.DMA((2,)), pltpu.SemaphoreType.DMA(())],
    )
    return pl.pallas_call(
        functools.partial(_expert_up_kernel, tr=tr, tf=tf, n_tiles_total=m // tr),
        grid_spec=grid_spec,
        out_shape=jax.ShapeDtypeStruct((m, f), BF16),
        compiler_params=_params("arbitrary", "arbitrary"),
        name=name,
    )(group_start, group_tiles, hs, w_gate, w_up)


def _kt_accumulate(tv_ref, a_ref, w_ref, acc_ref):
    k = pl.program_id(2)

    @pl.when(k == 0)
    def _():
        acc_ref[...] = jnp.zeros_like(acc_ref)

    @pl.when(tv_ref[pl.program_id(0)] != 0)
    def _():
        acc_ref[...] += _dot(a_ref[...], w_ref[0].astype(BF16))


def _kt_residual_kernel(te_ref, tv_ref, a_ref, w_ref, x_ref, g_ref, o_ref, acc_ref):
    _kt_accumulate(tv_ref, a_ref, w_ref, acc_ref)

    @pl.when(pl.program_id(2) == pl.num_programs(2) - 1)
    def _():
        o_ref[...] = x_ref[...] + g_ref[...] * acc_ref[...].reshape(o_ref.shape)


def _expert_down_kernel(iblk_ref, iexp_ref, ifirst_ref, ilast_ref, ivalid_ref, sube_ref, subv_ref,
                        a_ref, w_ref, rs_ref, o_ref, acc_ref, *, nsub, tr):
    it, k = pl.program_id(1), pl.program_id(2)
    blk = iblk_ref[it]
    e = iexp_ref[it]
    live = ivalid_ref[it] != 0

    @pl.when(jnp.logical_and(k == 0, ifirst_ref[it] != 0))
    def _():
        acc_ref[...] = jnp.zeros_like(acc_ref)

    mine = [jnp.logical_and(subv_ref[blk * nsub + s] != 0, sube_ref[blk * nsub + s] == e) for s in range(nsub)]
    whole = functools.reduce(jnp.logical_and, mine)

    @pl.when(jnp.logical_and(live, whole))
    def _():
        acc_ref[...] += _dot(a_ref[...], w_ref[0].astype(BF16))

    for s in range(nsub):
        @pl.when(jnp.logical_and(live, jnp.logical_and(jnp.logical_not(whole), mine[s])))
        def _(s=s):
            acc_ref[s * tr:(s + 1) * tr, :] += _dot(a_ref[s * tr:(s + 1) * tr, :], w_ref[0].astype(BF16))

    @pl.when(jnp.logical_and(k == pl.num_programs(2) - 1, ilast_ref[it] != 0))
    def _():
        o_ref[...] = rs_ref[0] * acc_ref[...]


def _matmul_residual(a, w, x3, gate3, tm, tn, tk, name):
    m, k = a.shape
    n = w.shape[-1]
    g, tg, _ = x3.shape
    gb = tm // tg
    nt = m // tm
    te = jnp.zeros((nt,), jnp.int32)
    tv = jnp.ones((nt,), jnp.int32)
    grid_spec = pltpu.PrefetchScalarGridSpec(
        num_scalar_prefetch=2,
        grid=(nt, n // tn, k // tk),
        in_specs=[pl.BlockSpec((tm, tk), lambda i, j, kk, te, tv: (i, kk)),
                  pl.BlockSpec((1, tk, tn), lambda i, j, kk, te, tv: (te[i], kk, j)),
                  pl.BlockSpec((gb, tg, tn), lambda i, j, kk, te, tv: (i, 0, j)),
                  pl.BlockSpec((gb, 1, tn), lambda i, j, kk, te, tv: (i, 0, j))],
        out_specs=pl.BlockSpec((gb, tg, tn), lambda i, j, kk, te, tv: (i, 0, j)),
        scratch_shapes=[pltpu.VMEM((tm, tn), F32)],
    )
    return pl.pallas_call(
        _kt_residual_kernel,
        grid_spec=grid_spec,
        out_shape=jax.ShapeDtypeStruct(x3.shape, F32),
        compiler_params=_params("arbitrary", "arbitrary", "arbitrary"),
        name=name,
    )(te, tv, a, w.reshape((1,) + w.shape[-2:]), x3, gate3)


def _expert_down_call(a, w, rowscale, items, tr, nsub, tn, tk, name):
    m, k = a.shape
    n = w.shape[-1]
    tb = tr * nsub
    nk = k // tk
    n_items = items[0].shape[0]

    def kk_of(it, kk, pf):
        return jnp.where(pf[4][it] != 0, kk, nk - 1)

    grid_spec = pltpu.PrefetchScalarGridSpec(
        num_scalar_prefetch=7,
        grid=(n // tn, n_items, nk),
        in_specs=[pl.BlockSpec((tb, tk), lambda j, it, kk, *pf: (pf[0][it], kk_of(it, kk, pf))),
                  pl.BlockSpec((1, tk, tn), lambda j, it, kk, *pf: (pf[1][it], kk_of(it, kk, pf), j)),
                  pl.BlockSpec((1, tb, 1), lambda j, it, kk, *pf: (pf[0][it], 0, 0))],
        out_specs=pl.BlockSpec((tb, tn), lambda j, it, kk, *pf: (pf[0][it], j)),
        scratch_shapes=[pltpu.VMEM((tb, tn), F32)],
    )
    return pl.pallas_call(
        functools.partial(_expert_down_kernel, nsub=nsub, tr=tr),
        grid_spec=grid_spec,
        out_shape=jax.ShapeDtypeStruct((m, n), F32),
        compiler_params=_params("arbitrary", "arbitrary", "arbitrary"),
        name=name,
    )(*items, a, w, rowscale.reshape(m // tb, tb, 1))


def _conv_kernel(x_ref, prev_ref, w_ref, b_ref, o_ref, xp_ref, *, tl, taps):
    t = pl.program_id(2)
    hist = taps - 1
    top = SUBLANES - hist

    @pl.when(t == 0)
    def _():
        xp_ref[top:SUBLANES, :] = prev_ref[0]

    @pl.when(t > 0)
    def _():
        xp_ref[0:SUBLANES, :] = xp_ref[tl:tl + SUBLANES, :]

    x = x_ref[...]
    xp_ref[SUBLANES:SUBLANES + tl, :] = x
    w = w_ref[...]
    acc = b_ref[...] + w[hist:taps] * x
    for k in range(hist):
        acc = acc + w[k:k + 1] * xp_ref[top + k:top + k + tl, :]
    o_ref[...] = _silu(acc)


def _conv_call(proj, row0, nb, seq, col0, conv_prev, conv_w, conv_b, name):
    taps, c = conv_w.shape
    tl = _tile(seq, 512, SUBLANES)
    tc = _tile(math.gcd(c, col0) if col0 else c, 1024)
    nt = seq // tl
    assert row0 % tl == 0 and col0 % tc == 0
    rb0, cb0 = row0 // tl, col0 // tc
    return pl.pallas_call(
        functools.partial(_conv_kernel, tl=tl, taps=taps),
        grid=(nb, c // tc, nt),
        in_specs=[pl.BlockSpec((tl, tc), lambda b, j, t: (rb0 + b * nt + t, cb0 + j)),
                  pl.BlockSpec((1, taps - 1, tc), lambda b, j, t: (b, 0, j)),
                  pl.BlockSpec((taps, tc), lambda b, j, t: (0, j)),
                  pl.BlockSpec((1, tc), lambda b, j, t: (0, j))],
        out_specs=pl.BlockSpec((tl, tc), lambda b, j, t: (b * nt + t, j)),
        out_shape=jax.ShapeDtypeStruct((nb * seq, c), F32),
        scratch_shapes=[pltpu.VMEM((tl + SUBLANES, tc), F32)],
        compiler_params=_params("arbitrary", "arbitrary", "arbitrary"),
        name=name,
    )(proj, conv_prev, conv_w, conv_b.reshape(1, c))


def _ssd_kernel(x_ref, b_ref, c_ref, dtr_ref, z_ref, sel_ref, e_ref, bias_ref, alog_ref, dskip_ref, nw_ref,
                s0_ref, g_ref, s_ref, st_ref, *, q, heads, hdim, gpb):
    c = pl.program_id(2)
    rp = heads * hdim
    n = b_ref.shape[1] // gpb
    per_lane = LANES // hdim

    @pl.when(c == 0)
    def _():
        for k in range(gpb):
            st_ref[k] = s0_ref[0, k].T

    dt_all = _softplus(dtr_ref[...] + bias_ref[...])
    da_all = dt_all * -jnp.exp(alog_ref[...])
    ri = lax.broadcasted_iota(jnp.int32, (q, q), 0)
    ci = lax.broadcasted_iota(jnp.int32, (q, q), 1)
    causal = ri >= ci
    tril = jnp.where(causal, 1.0, 0.0).astype(BF16)
    expand = e_ref[...]
    lane = lax.broadcasted_iota(jnp.int32, (q, LANES), 1)

    for k in range(gpb):
        cols = slice(k * rp, (k + 1) * rp)
        sel = sel_ref[k]
        dtg = _sel_dot(dt_all, sel)
        dag = _sel_dot(da_all, sel)
        acs = _sel_dot_left(tril, dag)
        acs_t = acs.T
        dt_exp = _sel_dot(dtg, expand)
        acs_exp = _sel_dot(acs, expand)

        x = x_ref[:, cols]
        xdt = x * dt_exp
        b_blk = b_ref[:, k * n:(k + 1) * n]
        bm = b_blk.astype(BF16)
        cm = c_ref[:, k * n:(k + 1) * n].astype(BF16)
        cb = _dot_nt(cm, bm)

        st = st_ref[k]
        y = _dot(cm, st.astype(BF16)) * jnp.exp(acs_exp)
        last = acs_exp[q - 1:q, :]
        xs = (xdt * jnp.exp(last - acs_exp)).astype(BF16)
        st_ref[k] = st * jnp.exp(last) + _dot(b_blk.T.astype(BF16), xs)

        pieces = []
        for pr in range(heads // per_lane):
            xp = xdt[:, pr * LANES:(pr + 1) * LANES]
            acc = None
            for hh in range(per_lane):
                r = pr * per_lane + hh
                seg = acs[:, r:r + 1] - acs_t[r:r + 1, :]
                dec = jnp.exp(jnp.where(causal, seg, -jnp.inf))
                m = (cb * dec).astype(BF16)
                in_head = jnp.logical_and(lane >= hh * hdim, lane < (hh + 1) * hdim)
                xm = jnp.where(in_head, xp, 0.0).astype(BF16)
                t = _dot(m, xm)
                acc = t if acc is None else acc + t
            pieces.append(acc)
        y = y + jnp.concatenate(pieces, axis=1) + dskip_ref[:, cols] * x

        gt = y * _silu(z_ref[:, cols])
        ms = jnp.mean(gt * gt, axis=-1, keepdims=True)
        g_ref[:, cols] = (gt * lax.rsqrt(ms + EPS) * nw_ref[:, cols]).astype(g_ref.dtype)

    @pl.when(c == pl.num_programs(2) - 1)
    def _():
        for k in range(gpb):
            s_ref[0, k] = st_ref[k].T


def _ssd_call(xbc, proj, dt_raw, row0, nb, seq, s0, consts, name):
    sel, expand, bias, alog, dskip, nw = consts
    groups, n_heads, _ = sel.shape
    _, _, rp, n = s0.shape
    heads = n_heads // groups
    hdim = rp // heads
    inner = groups * rp
    gpb = 4 if groups % 4 == 0 else (2 if groups % 2 == 0 else 1)
    q = SSD_CHUNK if seq % SSD_CHUNK == 0 else seq
    nc = seq // q
    assert row0 % q == 0 and LANES % hdim == 0 and heads % (LANES // hdim) == 0 and heads <= LANES
    assert (inner // n) % gpb == 0
    rb0 = row0 // q
    b_col0 = inner // (gpb * n)
    c_col0 = (inner + groups * n) // (gpb * n)
    return pl.pallas_call(
        functools.partial(_ssd_kernel, q=q, heads=heads, hdim=hdim, gpb=gpb),
        grid=(nb, groups // gpb, nc),
        in_specs=[pl.BlockSpec((q, gpb * rp), lambda b, g, c: (b * nc + c, g)),
                  pl.BlockSpec((q, gpb * n), lambda b, g, c: (b * nc + c, b_col0 + g)),
                  pl.BlockSpec((q, gpb * n), lambda b, g, c: (b * nc + c, c_col0 + g)),
                  pl.BlockSpec((q, n_heads), lambda b, g, c: (rb0 + b * nc + c, 0)),
                  pl.BlockSpec((q, gpb * rp), lambda b, g, c: (rb0 + b * nc + c, g)),
                  pl.BlockSpec((gpb, n_heads, LANES), lambda b, g, c: (g, 0, 0)),
                  pl.BlockSpec((LANES, rp), lambda b, g, c: (0, 0)),
                  pl.BlockSpec((1, n_heads), lambda b, g, c: (0, 0)),
                  pl.BlockSpec((1, n_heads), lambda b, g, c: (0, 0)),
                  pl.BlockSpec((1, gpb * rp), lambda b, g, c: (0, g)),
                  pl.BlockSpec((1, gpb * rp), lambda b, g, c: (0, g)),
                  pl.BlockSpec((1, gpb, rp, n), lambda b, g, c: (b, g, 0, 0))],
        out_specs=[pl.BlockSpec((q, gpb * rp), lambda b, g, c: (b * nc + c, g)),
                   pl.BlockSpec((1, gpb, rp, n), lambda b, g, c: (b, g, 0, 0))],
        out_shape=[jax.ShapeDtypeStruct((nb * seq, inner), BF16),
                   jax.ShapeDtypeStruct(s0.shape, F32)],
        scratch_shapes=[pltpu.VMEM((gpb, n, rp), F32)],
        compiler_params=_params("arbitrary", "arbitrary", "arbitrary"),
        name=name,
    )(xbc, xbc, xbc, dt_raw, proj, sel, expand, bias, alog, dskip, nw, s0)


def _sb_weights(z, carry, suffix, strict_mask):
    sub = suffix.shape[0]
    tail = jnp.log(1.0 + jnp.exp(-jnp.abs(z)))
    log_beta = jnp.minimum(z, 0.0) - tail
    log_keep = jnp.minimum(-z, 0.0) - tail
    if strict_mask is not None:
        log_keep = jnp.where(strict_mask, log_keep, 0.0)
    nsub = z.shape[1] // sub
    later = [None] * nsub
    run = carry
    for j in reversed(range(nsub)):
        lk = log_keep[:, j * sub:(j + 1) * sub]
        later[j] = _sel_dot(lk, suffix, parts=2) + run
        run = run + jnp.sum(lk, axis=1, keepdims=True)
    later = later[0] if nsub == 1 else jnp.concatenate(later, axis=1)
    w = jnp.exp(log_beta + later)
    if strict_mask is not None:
        w = jnp.where(strict_mask, w, 0.0)
    return w, run


def _suffix_matrix(n):
    ri = lax.broadcasted_iota(jnp.int32, (n, n), 0)
    ci = lax.broadcasted_iota(jnp.int32, (n, n), 1)
    return jnp.where(ri > ci, 1.0, 0.0).astype(BF16)


def _stacked_strict_mask(nh, t):
    ri = lax.broadcasted_iota(jnp.int32, (nh * t, t), 0)
    ci = lax.broadcasted_iota(jnp.int32, (nh * t, t), 1)
    return ci < jnp.bitwise_and(ri, t - 1)


def _sb_scores(q_ref, load_k, nh, hd, scale, h0=0):
    zs = [_dot_nt(q_ref[:, (h0 + h) * hd:(h0 + h + 1) * hd], load_k(h0 + h)) for h in range(nh)]
    return jnp.concatenate(zs, axis=0) * scale


def _sb_values(w, load_v, nh, t, h0=0):
    wb = w.astype(BF16)
    return jnp.concatenate([_dot(wb[h * t:(h + 1) * t], load_v(h0 + h)) for h in range(nh)], axis=0)


def _sb_prompt_kernel(q_ref, k_ref, v_ref, o_ref, acc_ref, carry_ref, *, blk, nh, hd, scale):
    qi = pl.program_id(2)
    suffix = _suffix_matrix(blk)

    def visit(kb, carry, mask):
        ks = pl.multiple_of(kb * blk, blk)
        z = _sb_scores(q_ref, lambda h: k_ref[pl.ds(ks, blk), h * hd:(h + 1) * hd], nh, hd, scale)
        w, run = _sb_weights(z, carry, suffix, mask)
        return _sb_values(w, lambda h: v_ref[pl.ds(ks, blk), h * hd:(h + 1) * hd], nh, blk), run

    out, run = visit(qi, jnp.zeros((nh * blk, 1), F32), _stacked_strict_mask(nh, blk))
    acc_ref[...] = out
    carry_ref[...] = run

    def body(i, c):
        out, run = visit(qi - 1 - i, carry_ref[...], None)
        acc_ref[...] += out
        carry_ref[...] = run
        return c

    lax.fori_loop(0, qi, body, 0)
    for h in range(nh):
        o_ref[:, h * hd:(h + 1) * hd] = acc_ref[h * blk:(h + 1) * blk, :].astype(o_ref.dtype)


def _heads_per_step(n_heads):
    return SUBLANES if n_heads % SUBLANES == 0 else n_heads


def _sb_prompt_call(q16, kv16, row0, nb, seq, n_heads, hd, scale, name):
    blk = SB_BLOCK
    nq = seq // blk
    nh = _heads_per_step(n_heads)
    ng = n_heads // nh
    assert row0 % seq == 0 and hd % LANES == 0 and blk & (blk - 1) == 0
    sb0 = row0 // seq
    rb0 = row0 // blk
    return pl.pallas_call(
        functools.partial(_sb_prompt_kernel, blk=blk, nh=nh, hd=hd, scale=scale),
        grid=(nb, ng, nq),
        in_specs=[pl.BlockSpec((blk, nh * hd), lambda b, g, i: (rb0 + b * nq + i, g)),
                  pl.BlockSpec((seq, nh * hd), lambda b, g, i: (sb0 + b, g)),
                  pl.BlockSpec((seq, nh * hd), lambda b, g, i: (sb0 + b, ng + g))],
        out_specs=pl.BlockSpec((blk, nh * hd), lambda b, g, i: (b * nq + i, g)),
        out_shape=jax.ShapeDtypeStruct((nb * seq, n_heads * hd), BF16),
        scratch_shapes=[pltpu.VMEM((nh * blk, hd), F32), pltpu.VMEM((nh * blk, 1), F32)],
        compiler_params=_params("arbitrary", "arbitrary", "arbitrary"),
        name=name,
    )(q16, kv16, kv16)


def _sb_decode_kernel(q_ref, kn_ref, vn_ref, kc_ref, vc_ref, o_ref, acc_ref, carry_ref, *, nh, hd, sub, scale):
    p = pl.program_id(2)
    t = q_ref.shape[0]

    @pl.when(p == 0)
    def _():
        z = _sb_scores(q_ref, lambda h: kn_ref[:, h * hd:(h + 1) * hd], nh, hd, scale)
        w, run = _sb_weights(z, jnp.zeros((nh * t, 1), F32), _suffix_matrix(t), _stacked_strict_mask(nh, t))
        acc_ref[...] = _sb_values(w, lambda h: vn_ref[:, h * hd:(h + 1) * hd], nh, t)
        carry_ref[...] = run

    pb = kc_ref.shape[0]
    k_rows = kc_ref.reshape(pb * nh, hd)
    v_rows = vc_ref.reshape(pb * nh, hd)
    z = _sb_scores(q_ref, lambda h: k_rows[pl.ds(h, pb, stride=nh), :].astype(BF16), nh, hd, scale)
    w, run = _sb_weights(z, carry_ref[...], _suffix_matrix(sub), None)
    acc_ref[...] += _sb_values(w, lambda h: v_rows[pl.ds(h, pb, stride=nh), :].astype(BF16), nh, t)
    carry_ref[...] = run

    @pl.when(p == pl.num_programs(2) - 1)
    def _():
        for h in range(nh):
            o_ref[:, h * hd:(h + 1) * hd] = acc_ref[h * t:(h + 1) * t, :].astype(o_ref.dtype)


def _sb_decode_call(q16, kv16, row0, nb, seq, n_heads, hd, cache_k, cache_v, scale, name):
    past = cache_k.shape[1]
    nh = _heads_per_step(n_heads)
    ng = n_heads // nh
    pb = _tile(past, 512)
    sub = _tile(pb, 256)
    npb = past // pb
    assert row0 % seq == 0 and seq & (seq - 1) == 0
    sb0 = row0 // seq
    return pl.pallas_call(
        functools.partial(_sb_decode_kernel, nh=nh, hd=hd, sub=sub, scale=scale),
        grid=(nb, ng, npb),
        in_specs=[pl.BlockSpec((seq, nh * hd), lambda b, g, p: (sb0 + b, g)),
                  pl.BlockSpec((seq, nh * hd), lambda b, g, p: (sb0 + b, g)),
                  pl.BlockSpec((seq, nh * hd), lambda b, g, p: (sb0 + b, ng + g)),
                  pl.BlockSpec((None, pb, nh, hd), lambda b, g, p: (b, npb - 1 - p, g, 0)),
                  pl.BlockSpec((None, pb, nh, hd), lambda b, g, p: (b, npb - 1 - p, g, 0))],
        out_specs=pl.BlockSpec((seq, nh * hd), lambda b, g, p: (b, g)),
        out_shape=jax.ShapeDtypeStruct((nb * seq, n_heads * hd), BF16),
        scratch_shapes=[pltpu.VMEM((nh * seq, hd), F32), pltpu.VMEM((nh * seq, 1), F32)],
        compiler_params=_params("arbitrary", "arbitrary", "arbitrary"),
        name=name,
    )(q16, kv16, kv16, cache_k, cache_v)


def _gather_cast_kernel(src_ref, h_ref, o_ref, buf_ref, sem):
    tm = buf_ref.shape[0]

    def row_copy(r):
        return pltpu.make_async_copy(h_ref.at[pl.ds(src_ref[0, 0, r], 1), :], buf_ref.at[pl.ds(r, 1), :], sem)

    def start(r2, carry):
        for thread in range(2):
            row_copy(2 * r2 + thread).start(priority=thread)
        return carry

    def wait(r, carry):
        row_copy(r).wait()
        return carry

    lax.fori_loop(0, tm // 2, start, 0)
    lax.fori_loop(0, tm, wait, 0)
    o_ref[...] = buf_ref[...].astype(o_ref.dtype)


def _gather_cast_call(h, src_rows, tm):
    m, d = h.shape
    nt = src_rows.shape[0] // tm
    return pl.pallas_call(
        _gather_cast_kernel,
        grid=(nt,),
        in_specs=[pl.BlockSpec((1, 1, tm), lambda i: (i, 0, 0), memory_space=pltpu.SMEM),
                  pl.BlockSpec(memory_space=pl.ANY)],
        out_specs=pl.BlockSpec((tm, d), lambda i: (i, 0)),
        out_shape=jax.ShapeDtypeStruct((nt * tm, d), BF16),
        scratch_shapes=[pltpu.VMEM((tm, d), F32), pltpu.SemaphoreType.DMA(())],
        compiler_params=_params("arbitrary"),
        name="moe_gather_rows",
    )(src_rows.reshape(nt, 1, tm), h)


def _combine_kernel(p1_ref, p2_ref, ys_ref, x_ref, g_ref, o_ref, buf_ref, sem):
    tm = buf_ref.shape[1]

    def copies(r):
        return (pltpu.make_async_copy(ys_ref.at[pl.ds(p1_ref[0, 0, r], 1), :], buf_ref.at[0, pl.ds(r, 1), :], sem),
                pltpu.make_async_copy(ys_ref.at[pl.ds(p2_ref[0, 0, r], 1), :], buf_ref.at[1, pl.ds(r, 1), :], sem))

    def start(r, carry):
        for thread, cp in enumerate(copies(r)):
            cp.start(priority=thread)
        return carry

    def wait(r, carry):
        for cp in copies(r):
            cp.wait()
        return carry

    lax.fori_loop(0, tm, start, 0)
    lax.fori_loop(0, tm, wait, 0)
    f = (buf_ref[0] + buf_ref[1]).reshape(o_ref.shape)
    o_ref[...] = x_ref[...] + g_ref[...] * f


def _combine_call(ys, pos1, pos2, x3, gate3, tm):
    g, tg, d = x3.shape
    gb = tm // tg
    nt = (g * tg) // tm
    return pl.pallas_call(
        _combine_kernel,
        grid=(nt,),
        in_specs=[pl.BlockSpec((1, 1, tm), lambda i: (i, 0, 0), memory_space=pltpu.SMEM),
                  pl.BlockSpec((1, 1, tm), lambda i: (i, 0, 0), memory_space=pltpu.SMEM),
                  pl.BlockSpec(memory_space=pl.ANY),
                  pl.BlockSpec((gb, tg, d), lambda i: (i, 0, 0)),
                  pl.BlockSpec((gb, 1, d), lambda i: (i, 0, 0))],
        out_specs=pl.BlockSpec((gb, tg, d), lambda i: (i, 0, 0)),
        out_shape=jax.ShapeDtypeStruct(x3.shape, F32),
        scratch_shapes=[pltpu.VMEM((2, tm, d), F32), pltpu.SemaphoreType.DMA(())],
        compiler_params=_params("arbitrary"),
        name="moe_combine_rows",
    )(pos1.reshape(nt, 1, tm), pos2.reshape(nt, 1, tm), ys, x3, gate3)


def _final_norm_kernel(x_ref, w_ref, o_ref):
    x = x_ref[...]
    ms = jnp.mean(x * x, axis=-1, keepdims=True)
    o_ref[...] = x * lax.rsqrt(ms + EPS) * w_ref[...]


def _final_norm_call(x2, w, row0, n_rows, name):
    _, d = x2.shape
    tm = _tile(math.gcd(n_rows, row0) if row0 else n_rows, 256, SUBLANES)
    rb0 = row0 // tm
    return pl.pallas_call(
        _final_norm_kernel,
        grid=(n_rows // tm,),
        in_specs=[pl.BlockSpec((tm, d), lambda i: (rb0 + i, 0)), pl.BlockSpec((1, d), lambda i: (0, 0))],
        out_specs=pl.BlockSpec((tm, d), lambda i: (i, 0)),
        out_shape=jax.ShapeDtypeStruct((n_rows, d), F32),
        compiler_params=_params("arbitrary"),
        name=name,
    )(x2, w.reshape(1, d))


def _ssd_constants(groups, n_heads, hdim, dt_bias, a_log, d_skip, norm_w):
    heads = n_heads // groups
    sel = np.zeros((groups, n_heads, LANES), np.float32)
    for g in range(groups):
        for r in range(heads):
            sel[g, g * heads + r, r] = 1.0
    expand = np.zeros((LANES, heads * hdim), np.float32)
    for r in range(heads):
        expand[r, r * hdim:(r + 1) * hdim] = 1.0
    return (jnp.asarray(sel, BF16), jnp.asarray(expand, BF16),
            dt_bias.astype(F32).reshape(1, n_heads), a_log.astype(F32).reshape(1, n_heads),
            jnp.repeat(d_skip.astype(F32), hdim).reshape(1, n_heads * hdim), norm_w.reshape(1, n_heads * hdim))


def _route(route, n_experts, tr, nsub):
    m = route.shape[0]
    idx = route[:, :TOP_K].astype(jnp.int32)
    gates = route[:, TOP_K:2 * TOP_K]
    flat_e = idx.reshape(-1)
    n_assign = flat_e.shape[0]
    n_tiles = -(-(n_assign // tr + n_experts) // nsub) * nsub
    n_blocks = n_tiles // nsub
    order = jnp.argsort(flat_e, stable=True)
    counts = jnp.zeros((n_experts,), jnp.int32).at[flat_e].add(1)
    tiles = (counts + tr - 1) // tr
    padded = tiles * tr
    pad_end = jnp.cumsum(padded)
    pad_start = pad_end - padded
    cnt_start = jnp.cumsum(counts) - counts
    sorted_e = flat_e[order]
    dest_sorted = pad_start[sorted_e] + (jnp.arange(n_assign, dtype=jnp.int32) - cnt_start[sorted_e])
    dest = jnp.zeros((n_assign,), jnp.int32).at[order].set(dest_sorted)
    src_rows = jnp.zeros((n_tiles * tr,), jnp.int32).at[dest].set(jnp.arange(n_assign, dtype=jnp.int32) // TOP_K)
    row_gate = jnp.zeros((n_tiles * tr,), F32).at[dest].set(gates.reshape(-1))
    dest2 = dest.reshape(m, TOP_K)

    t_idx = jnp.arange(n_tiles, dtype=jnp.int32)
    t_start = t_idx * tr
    sub_e = jnp.minimum(jnp.searchsorted(pad_end, t_start, side="right"), n_experts - 1).astype(jnp.int32)
    sub_v = t_start < pad_end[-1]
    prev_e = jnp.concatenate([jnp.full((1,), -1, jnp.int32), sub_e[:-1]])
    run_start = jnp.logical_or(t_idx % nsub == 0, jnp.logical_and(sub_v, sub_e != prev_e))
    n_items = n_blocks + n_experts - 1
    item_sub = jnp.nonzero(run_start, size=n_items, fill_value=-1)[0].astype(jnp.int32)
    item_exists = item_sub >= 0
    n_exist = jnp.sum(item_exists.astype(jnp.int32))
    item_sub = jnp.where(item_exists, item_sub, item_sub[jnp.maximum(n_exist - 1, 0)])
    item_live = jnp.logical_and(item_exists, sub_v[item_sub])
    item_blk = item_sub // nsub
    item_exp = sub_e[item_sub]
    item_first = jnp.logical_and(item_exists, item_sub % nsub == 0)
    next_blk = jnp.concatenate([item_blk[1:], jnp.full((1,), -1, jnp.int32)])
    next_exists = jnp.concatenate([item_exists[1:], jnp.zeros((1,), bool)])
    item_last = jnp.logical_and(item_exists, jnp.logical_or(jnp.logical_not(next_exists), next_blk != item_blk))

    def i32(v):
        return v.astype(jnp.int32)

    items = (i32(item_blk), i32(item_exp), i32(item_first), i32(item_last), i32(item_live), sub_e, i32(sub_v))
    return src_rows, row_gate, i32(pad_start), i32(tiles), items, dest2[:, 0], dest2[:, 1]


def kernel(x_prompt, x_sample, state_ssm, state_conv, cache_k, cache_v, c_prompt, c_sample, w_mod, b_mod, ssd_w_in, ssd_conv_w, ssd_conv_b, ssd_dt_bias, ssd_a_log, ssd_d, ssd_norm_w, ssd_w_out, sb_w_qkv, sb_w_o, ffn_w_gate, ffn_w_up, ffn_w_down, moe_w_router, moe_w_gate, moe_w_up, moe_w_down, final_norm_w):
    bp, lp, d = x_prompt.shape
    bs, ls, _ = x_sample.shape
    depth = w_mod.shape[0]
    mp, msamp = bp * lp, bs * ls
    m = mp + msamp
    tg = math.gcd(lp, ls)
    assert tg % SUBLANES == 0
    n_groups = m // tg
    n_heads = ssd_a_log.shape[1]
    hdim, n_state = state_ssm.shape[3], state_ssm.shape[4]
    inner = n_heads * hdim
    conv_dim = ssd_conv_w.shape[2]
    ssd_groups = (conv_dim - inner) // (2 * n_state)
    sb_heads, sb_hd = cache_k.shape[3], cache_k.shape[4]
    n_experts = moe_w_router.shape[2]
    d_ff = ffn_w_gate.shape[2]
    tm = _tile(m, 1024, tg)

    seq_of_group = np.concatenate([np.repeat(np.arange(bp), lp // tg), bp + np.repeat(np.arange(bs), ls // tg)])
    n_seq = bp + bs
    rows = -(-n_seq // 16) * 16
    c_all = jnp.zeros((rows, d), F32).at[:n_seq].set(jnp.concatenate([c_prompt, c_sample], axis=0))
    mod = _mod_call(c_all, w_mod, b_mod)
    mod_g = mod[:, seq_of_group, :].reshape(depth, n_groups, 1, 6, d)

    def mvec(i, which):
        return mod_g[i, :, :, which, :]

    x3 = jnp.concatenate([x_prompt.reshape(mp, d), x_sample.reshape(msamp, d)], axis=0).reshape(n_groups, tg, d)
    zeros_ssm = jnp.zeros((bp,) + state_ssm.shape[2:], F32)
    zeros_conv = jnp.zeros((bp,) + state_conv.shape[2:], F32)
    ssm_p, conv_p, k_p, v_p, ssm_s, conv_s, k_s, v_s = [], [], [], [], [], [], [], []

    for i in range(depth):
        j = i // 2
        h = _modulate_call(x3, mvec(i, 0), mvec(i, 1), BF16).reshape(m, d)
        if i % 2 == 0:
            w_in = ssd_w_in[j]
            n_main = inner + conv_dim
            proj, = _matmul_ws(h, w_in, 0, n_main, _tile(n_main, 512), tm, "ssd_in_proj")
            dt_raw, = _matmul_ws(h, w_in, n_main, n_heads, n_heads, tm, "ssd_in_proj_dt")
            consts = _ssd_constants(ssd_groups, n_heads, hdim, ssd_dt_bias[j], ssd_a_log[j], ssd_d[j], ssd_norm_w[j])
            outs = []
            for (row0, nb, seq, prev, s0, conv_out, ssm_out, tag) in (
                    (0, bp, lp, zeros_conv, zeros_ssm, conv_p, ssm_p, "prompt"),
                    (mp, bs, ls, state_conv[j], state_ssm[j], conv_s, ssm_s, "sample")):
                xbc = _conv_call(proj, row0, nb, seq, inner, prev, ssd_conv_w[j], ssd_conv_b[j], "ssd_conv_" + tag)
                s0g = s0.astype(F32).reshape(nb, ssd_groups, (n_heads // ssd_groups) * hdim, n_state)
                g_out, s_fin = _ssd_call(xbc, proj, dt_raw, row0, nb, seq, s0g, consts, "ssd_scan_" + tag)
                outs.append(g_out)
                ssm_out.append(s_fin.reshape(nb, n_heads, hdim, n_state))
                hist = ssd_conv_w.shape[1] - 1
                last_rows = (row0 + np.arange(nb)[:, None] * seq + np.arange(seq - hist, seq)[None, :]).reshape(-1)
                conv_out.append(jnp.take(proj, last_rows, axis=0)[:, inner:].reshape(nb, hist, conv_dim))
            mix = jnp.concatenate(outs, axis=0)
            w_mix = ssd_w_out[j]
        else:
            hd_all = sb_heads * sb_hd
            tn = _tile(hd_all, 512)
            q16, = _matmul_ws(h, sb_w_qkv[j], 0, hd_all, tn, tm, "sb_q", (BF16,))
            kv32, kv16 = _matmul_ws(h, sb_w_qkv[j], hd_all, 2 * hd_all, tn, tm, "sb_kv", (F32, BF16))
            scale = float(sb_hd) ** -0.5
            o_p = _sb_prompt_call(q16, kv16, 0, bp, lp, sb_heads, sb_hd, scale, "sb_attn_prompt")
            o_s = _sb_decode_call(q16, kv16, mp, bs, ls, sb_heads, sb_hd, cache_k[j], cache_v[j], scale,
                                  "sb_attn_sample")
            mix = jnp.concatenate([o_p, o_s], axis=0)
            w_mix = sb_w_o[j]
            for (row0, nb, seq, k_out, v_out) in ((0, bp, lp, k_p, v_p), (mp, bs, ls, k_s, v_s)):
                blk = kv32[row0:row0 + nb * seq]
                k_out.append(blk[:, :hd_all].reshape(nb, seq, sb_heads, sb_hd))
                v_out.append(blk[:, hd_all:].reshape(nb, seq, sb_heads, sb_hd))
        kmix = mix.shape[1]
        x3 = _matmul_residual(mix, w_mix, x3, mvec(i, 2), tm, _tile(d, 1024), _tile(kmix, 1024), "mixer_out_proj")

        if i % 2 == 0:
            h = _modulate_call(x3, mvec(i, 3), mvec(i, 4), BF16).reshape(m, d)
            nt = m // tm
            act = _swiglu_call(h, ffn_w_gate[j][None], ffn_w_up[j][None], jnp.zeros((nt,), jnp.int32),
                               jnp.ones((nt,), jnp.int32), tm, _tile(d_ff, 256), "ffn_up")
            x3 = _matmul_residual(act, ffn_w_down[j], x3, mvec(i, 5), tm, _tile(d, 1024), _tile(d_ff, 1024), "ffn_down")
        else:
            h32, route = _modulate_router_call(x3, mvec(i, 3), mvec(i, 4), moe_w_router[j])
            tr = _tile(m, 256, tg)
            nsub = 4
            src_rows, row_gate, group_start, group_tiles, items, pos1, pos2 = _route(
                route.reshape(m, LANES), n_experts, tr, nsub)
            hs = _gather_cast_call(h32.reshape(m, d), src_rows, nsub * tr // 2)
            act = _expert_up_call(hs, moe_w_gate[j], moe_w_up[j], group_start, group_tiles, tr, _tile(d_ff, 512),
                                  "moe_up")
            ys = _expert_down_call(act, moe_w_down[j], row_gate, items, tr, nsub, _tile(d, 2048), _tile(d_ff, 1024),
                                   "moe_down")
            x3 = _combine_call(ys, pos1, pos2, x3, mvec(i, 5), _tile(m, 256, tg))

    x2 = x3.reshape(m, d)
    y_prompt = _final_norm_call(x2, final_norm_w, 0, mp, "final_norm_prompt").reshape(bp, lp, d)
    y_sample = _final_norm_call(x2, final_norm_w, mp, msamp, "final_norm_sample").reshape(bs, ls, d)
    return (y_prompt, y_sample, jnp.stack(ssm_p), jnp.stack(conv_p), jnp.stack(k_p), jnp.stack(v_p),
            jnp.stack(ssm_s), jnp.stack(conv_s), jnp.stack(k_s), jnp.stack(v_s))
```

```python
import functools
import math

import numpy as np
import jax
import jax.numpy as jnp
from jax import lax
from jax.experimental import pallas as pl
from jax.experimental.pallas import tpu as pltpu

F32 = jnp.float32
BF16 = jnp.bfloat16
EPS = 1e-6
SSD_CHUNK = 64
SB_BLOCK = 128
TOP_K = 2
LANES = 128
SUBLANES = 8
VMEM_LIMIT = 56 * 1024 * 1024
ROW_DMA_THREAD = 1


def _tile(n, pref, align=LANES):
    t = min(pref, n)
    t -= t % align
    while t >= align:
        if n % t == 0:
            return t
        t -= align
    return n


def _params(*sem):
    return pltpu.CompilerParams(dimension_semantics=sem, vmem_limit_bytes=VMEM_LIMIT)


def _sigmoid(x):
    return 1.0 / (1.0 + jnp.exp(-x))


def _silu(x):
    return x * _sigmoid(x)


def _softplus(x):
    return jnp.maximum(x, 0.0) + jnp.log1p(jnp.exp(-jnp.abs(x)))


def _split_bf16(x, parts):
    out = []
    r = x
    for _ in range(parts - 1):
        p = r.astype(BF16)
        out.append(p)
        r = r - p.astype(F32)
    out.append(r.astype(BF16))
    return out


def _dot(a, b):
    return jnp.dot(a, b, preferred_element_type=F32)


def _dot_nt(a, b):
    return lax.dot_general(a, b, (((1,), (1,)), ((), ())), preferred_element_type=F32)


def _sel_dot(x, onehot, parts=3):
    acc = None
    for p in _split_bf16(x, parts):
        t = _dot(p, onehot)
        acc = t if acc is None else acc + t
    return acc


def _sel_dot_left(onehot, x, parts=3):
    acc = None
    for p in _split_bf16(x, parts):
        t = _dot(onehot, p)
        acc = t if acc is None else acc + t
    return acc


def _mod_kernel(c_ref, w_ref, b_ref, o_ref):
    cs = _silu(c_ref[...]).astype(BF16)
    o_ref[0] = _dot(cs, w_ref[0].astype(BF16)) + b_ref[0]


def _mod_call(c_pad, w_mod, b_mod):
    depth, d, n = w_mod.shape
    rows = c_pad.shape[0]
    tn = _tile(n, 512)
    return pl.pallas_call(
        _mod_kernel,
        grid=(depth, n // tn),
        in_specs=[pl.BlockSpec((rows, d), lambda i, j: (0, 0)),
                  pl.BlockSpec((1, d, tn), lambda i, j: (i, 0, j)),
                  pl.BlockSpec((1, 1, tn), lambda i, j: (i, 0, j))],
        out_specs=pl.BlockSpec((1, rows, tn), lambda i, j: (i, 0, j)),
        out_shape=jax.ShapeDtypeStruct((depth, rows, n), F32),
        compiler_params=_params("arbitrary", "arbitrary"),
        name="adaln_mod",
    )(c_pad, w_mod, b_mod.reshape(depth, 1, n))


def _modulate_kernel(x_ref, sh_ref, sc_ref, o_ref):
    x = x_ref[...]
    ms = jnp.mean(x * x, axis=-1, keepdims=True)
    h = x * lax.rsqrt(ms + EPS) * (1.0 + sc_ref[...]) + sh_ref[...]
    o_ref[...] = h.astype(o_ref.dtype)


def _modulate_call(x3, sh, sc, out_dtype):
    g, tg, d = x3.shape
    gb = _tile(g, max(1, 256 // tg), 1)
    return pl.pallas_call(
        _modulate_kernel,
        grid=(g // gb,),
        in_specs=[pl.BlockSpec((gb, tg, d), lambda i: (i, 0, 0)),
                  pl.BlockSpec((gb, 1, d), lambda i: (i, 0, 0)),
                  pl.BlockSpec((gb, 1, d), lambda i: (i, 0, 0))],
        out_specs=pl.BlockSpec((gb, tg, d), lambda i: (i, 0, 0)),
        out_shape=jax.ShapeDtypeStruct((g, tg, d), out_dtype),
        compiler_params=_params("arbitrary"),
        name="adaln_modulate",
    )(x3, sh, sc)


def _modulate_router_kernel(x_ref, sh_ref, sc_ref, wr_ref, o_ref, r_ref, *, n_experts):
    x = x_ref[...]
    gb, tg, d = x.shape
    ms = jnp.mean(x * x, axis=-1, keepdims=True)
    h = x * lax.rsqrt(ms + EPS) * (1.0 + sc_ref[...]) + sh_ref[...]
    o_ref[...] = h
    h2 = h.reshape(gb * tg, d)
    h_hi, h_lo = _split_bf16(h2, 2)
    w_hi, w_lo = _split_bf16(wr_ref[...], 2)
    logits = _dot(h_hi, w_hi) + _dot(h_hi, w_lo) + _dot(h_lo, w_hi)
    lane = lax.broadcasted_iota(jnp.int32, logits.shape, 1).astype(F32)
    neg = jnp.float32(-jnp.inf)
    lg = jnp.where(lane < n_experts, logits, neg)
    m1 = jnp.max(lg, axis=1, keepdims=True)
    i1 = jnp.min(jnp.where(lg == m1, lane, float(LANES)), axis=1, keepdims=True)
    lg2 = jnp.where(lane == i1, neg, lg)
    m2 = jnp.max(lg2, axis=1, keepdims=True)
    i2 = jnp.min(jnp.where(lg2 == m2, lane, float(LANES)), axis=1, keepdims=True)
    e2 = jnp.exp(m2 - m1)
    g1 = 1.0 / (1.0 + e2)
    g2 = e2 / (1.0 + e2)
    out = jnp.where(lane == 0.0, i1,
                    jnp.where(lane == 1.0, i2,
                              jnp.where(lane == 2.0, g1, jnp.where(lane == 3.0, g2, 0.0))))
    r_ref[...] = out.reshape(gb, tg, LANES)


def _modulate_router_call(x3, sh, sc, w_router):
    g, tg, d = x3.shape
    n_experts = w_router.shape[1]
    wr = jnp.zeros((d, LANES), F32).at[:, :n_experts].set(w_router)
    gb = _tile(g, max(1, 256 // tg), 1)
    return pl.pallas_call(
        functools.partial(_modulate_router_kernel, n_experts=n_experts),
        grid=(g // gb,),
        in_specs=[pl.BlockSpec((gb, tg, d), lambda i: (i, 0, 0)),
                  pl.BlockSpec((gb, 1, d), lambda i: (i, 0, 0)),
                  pl.BlockSpec((gb, 1, d), lambda i: (i, 0, 0)),
                  pl.BlockSpec((d, LANES), lambda i: (0, 0))],
        out_specs=[pl.BlockSpec((gb, tg, d), lambda i: (i, 0, 0)),
                   pl.BlockSpec((gb, tg, LANES), lambda i: (i, 0, 0))],
        out_shape=[jax.ShapeDtypeStruct((g, tg, d), F32),
                   jax.ShapeDtypeStruct((g, tg, LANES), F32)],
        compiler_params=_params("arbitrary"),
        name="adaln_modulate_router",
    )(x3, sh, sc, wr)


def _ws_kernel(a_ref, w_ref, *rest):
    o_refs, wb_ref = rest[:-1], rest[-1]

    @pl.when(pl.program_id(1) == 0)
    def _():
        wb_ref[...] = w_ref[...].astype(BF16)

    acc = _dot(a_ref[...], wb_ref[...])
    for o_ref in o_refs:
        o_ref[...] = acc.astype(o_ref.dtype)


def _matmul_ws(a, w, col_off, n_cols, tn, tm, name, out_dtypes=(F32,)):
    m, k = a.shape
    off = col_off // tn
    assert col_off % tn == 0 and n_cols % tn == 0 and m % tm == 0
    return pl.pallas_call(
        _ws_kernel,
        grid=(n_cols // tn, m // tm),
        in_specs=[pl.BlockSpec((tm, k), lambda n, i: (i, 0)),
                  pl.BlockSpec((k, tn), lambda n, i: (0, n + off))],
        out_specs=[pl.BlockSpec((tm, tn), lambda n, i: (i, n)) for _ in out_dtypes],
        out_shape=[jax.ShapeDtypeStruct((m, n_cols), dt) for dt in out_dtypes],
        scratch_shapes=[pltpu.VMEM((k, tn), BF16)],
        compiler_params=_params("arbitrary", "arbitrary"),
        name=name,
    )(a, w)


def _swiglu_kernel(te_ref, tv_ref, a_ref, wg_ref, wu_ref, o_ref, wgb_ref, wub_ref):
    i = pl.program_id(1)
    fresh = jnp.logical_or(i == 0, te_ref[i] != te_ref[jnp.maximum(i - 1, 0)])

    @pl.when(fresh)
    def _():
        wgb_ref[...] = wg_ref[0].astype(BF16)
        wub_ref[...] = wu_ref[0].astype(BF16)

    @pl.when(tv_ref[i] != 0)
    def _():
        a = a_ref[...]
        g = _dot(a, wgb_ref[...])
        u = _dot(a, wub_ref[...])
        o_ref[...] = (_silu(g) * u).astype(o_ref.dtype)

    @pl.when(tv_ref[i] == 0)
    def _():
        o_ref[...] = jnp.zeros_like(o_ref)


def _swiglu_call(a, w_gate, w_up, tile_expert, tile_valid, tm, tf, name):
    m, k = a.shape
    _, _, f = w_gate.shape
    nt = m // tm
    grid_spec = pltpu.PrefetchScalarGridSpec(
        num_scalar_prefetch=2,
        grid=(f // tf, nt),
        in_specs=[pl.BlockSpec((tm, k), lambda j, i, te, tv: (i, 0)),
                  pl.BlockSpec((1, k, tf), lambda j, i, te, tv: (te[i], 0, j)),
                  pl.BlockSpec((1, k, tf), lambda j, i, te, tv: (te[i], 0, j))],
        out_specs=pl.BlockSpec((tm, tf), lambda j, i, te, tv: (i, j)),
        scratch_shapes=[pltpu.VMEM((k, tf), BF16), pltpu.VMEM((k, tf), BF16)],
    )
    return pl.pallas_call(
        _swiglu_kernel,
        grid_spec=grid_spec,
        out_shape=jax.ShapeDtypeStruct((m, f), BF16),
        compiler_params=_params("arbitrary", "arbitrary"),
        name=name,
    )(tile_expert, tile_valid, a, w_gate, w_up)


def _expert_up_kernel(start_ref, count_ref, hs_ref, wg_ref, wu_ref, act_ref,
                      wgb_ref, wub_ref, a_buf, o_buf, zero_buf, a_sem, o_sem, z_sem, *, tr, tf, n_tiles_total):
    j, e = pl.program_id(0), pl.program_id(1)
    n = count_ref[e]
    row0 = start_ref[e]
    col0 = pl.multiple_of(j * tf, tf)
    full = 2 * tr
    n_full = n // 2
    odd = n - 2 * n_full
    n_chunks = n_full + odd

    def rows(c, size):
        return pl.ds(pl.multiple_of(row0 + c * full, tr), size)

    def a_copy(c, slot, size):
        return pltpu.make_async_copy(hs_ref.at[rows(c, size), :], a_buf.at[slot, pl.ds(0, size), :], a_sem.at[slot])

    def o_copy(c, slot, size):
        return pltpu.make_async_copy(o_buf.at[slot, pl.ds(0, size), :], act_ref.at[rows(c, size), pl.ds(col0, tf)],
                                     o_sem.at[slot])

    def fetch(c, slot):
        @pl.when(c < n_full)
        def _():
            a_copy(c, slot, full).start(priority=ROW_DMA_THREAD)

        @pl.when(jnp.logical_and(c == n_full, odd == 1))
        def _():
            a_copy(c, slot, tr).start(priority=ROW_DMA_THREAD)

    def consume(c, slot, size):
        a_copy(c, slot, size).wait()
        fetch(c + 1, 1 - slot)
        a = a_buf[slot, pl.ds(0, size), :]
        g = _dot(a, wgb_ref[...])
        u = _dot(a, wub_ref[...])

        @pl.when(c >= 2)
        def _():
            o_copy(c - 2, slot, full).wait()

        o_buf[slot, pl.ds(0, size), :] = (_silu(g) * u).astype(o_buf.dtype)
        o_copy(c, slot, size).start(priority=ROW_DMA_THREAD)

    fetch(0, 0)
    wgb_ref[...] = wg_ref[0].astype(BF16)
    wub_ref[...] = wu_ref[0].astype(BF16)

    def body(c, carry):
        consume(c, lax.rem(c, 2), full)
        return carry

    lax.fori_loop(0, n_full, body, 0)

    @pl.when(odd == 1)
    def _():
        consume(n_full, lax.rem(n_full, 2), tr)

    @pl.when(n_chunks >= 2)
    def _():
        o_copy(n_chunks - 2, lax.rem(n_chunks, 2), full).wait()

    @pl.when(jnp.logical_and(n_chunks >= 1, odd == 1))
    def _():
        o_copy(n_chunks - 1, lax.rem(n_chunks + 1, 2), tr).wait()

    @pl.when(jnp.logical_and(n_chunks >= 1, odd == 0))
    def _():
        o_copy(n_chunks - 1, lax.rem(n_chunks + 1, 2), full).wait()

    @pl.when(e == pl.num_programs(1) - 1)
    def _():
        first = (row0 + n * tr) // tr
        zero_buf[...] = jnp.zeros_like(zero_buf)

        def z_copy(t):
            dst = act_ref.at[pl.ds(pl.multiple_of(t * tr, tr), tr), pl.ds(col0, tf)]
            return pltpu.make_async_copy(zero_buf, dst, z_sem)

        def z_start(t, carry):
            z_copy(t).start()
            return carry

        def z_wait(t, carry):
            z_copy(t).wait()
            return carry

        lax.fori_loop(first, n_tiles_total, z_start, 0)
        lax.fori_loop(first, n_tiles_total, z_wait, 0)


def _expert_up_call(hs, w_gate, w_up, group_start, group_tiles, tr, tf, name):
    m, k = hs.shape
    n_experts, _, f = w_gate.shape
    grid_spec = pltpu.PrefetchScalarGridSpec(
        num_scalar_prefetch=2,
        grid=(f // tf, n_experts),
        in_specs=[pl.BlockSpec(memory_space=pl.ANY),
                  pl.BlockSpec((1, k, tf), lambda j, e, st, ct: (e, 0, j)),
                  pl.BlockSpec((1, k, tf), lambda j, e, st, ct: (e, 0, j))],
        out_specs=pl.BlockSpec(memory_space=pl.ANY),
        scratch_shapes=[pltpu.VMEM((k, tf), BF16), pltpu.VMEM((k, tf), BF16),
                        pltpu.VMEM((2, 2 * tr, k), BF16), pltpu.VMEM((2, 2 * tr, tf), BF16), pltpu.VMEM((tr, tf), BF16),
                        pltpu.SemaphoreType.DMA((2,)), pltpu.SemaphoreType.DMA((2,)), pltpu.SemaphoreType.DMA(())],
    )
    return pl.pallas_call(
        functools.partial(_expert_up_kernel, tr=tr, tf=tf, n_tiles_total=m // tr),
        grid_spec=grid_spec,
        out_shape=jax.ShapeDtypeStruct((m, f), BF16),
        compiler_params=_params("arbitrary", "arbitrary"),
        name=name,
    )(group_start, group_tiles, hs, w_gate, w_up)


def _kt_accumulate(tv_ref, a_ref, w_ref, acc_ref):
    k = pl.program_id(2)

    @pl.when(k == 0)
    def _():
        acc_ref[...] = jnp.zeros_like(acc_ref)

    @pl.when(tv_ref[pl.program_id(0)] != 0)
    def _():
        acc_ref[...] += _dot(a_ref[...], w_ref[0].astype(BF16))


def _kt_residual_kernel(te_ref, tv_ref, a_ref, w_ref, x_ref, g_ref, o_ref, acc_ref):
    _kt_accumulate(tv_ref, a_ref, w_ref, acc_ref)

    @pl.when(pl.program_id(2) == pl.num_programs(2) - 1)
    def _():
        o_ref[...] = x_ref[...] + g_ref[...] * acc_ref[...].reshape(o_ref.shape)


def _expert_down_kernel(iblk_ref, iexp_ref, ifirst_ref, ilast_ref, ivalid_ref, sube_ref, subv_ref,
                        a_ref, w_ref, rs_ref, o_ref, acc_ref, *, nsub, tr):
    it, k = pl.program_id(1), pl.program_id(2)
    blk = iblk_ref[it]
    e = iexp_ref[it]
    live = ivalid_ref[it] != 0

    @pl.when(jnp.logical_and(k == 0, ifirst_ref[it] != 0))
    def _():
        acc_ref[...] = jnp.zeros_like(acc_ref)

    mine = [jnp.logical_and(subv_ref[blk * nsub + s] != 0, sube_ref[blk * nsub + s] == e) for s in range(nsub)]
    whole = functools.reduce(jnp.logical_and, mine)

    @pl.when(jnp.logical_and(live, whole))
    def _():
        acc_ref[...] += _dot(a_ref[...], w_ref[0].astype(BF16))

    for s in range(nsub):
        @pl.when(jnp.logical_and(live, jnp.logical_and(jnp.logical_not(whole), mine[s])))
        def _(s=s):
            acc_ref[s * tr:(s + 1) * tr, :] += _dot(a_ref[s * tr:(s + 1) * tr, :], w_ref[0].astype(BF16))

    @pl.when(jnp.logical_and(k == pl.num_programs(2) - 1, ilast_ref[it] != 0))
    def _():
        o_ref[...] = rs_ref[0] * acc_ref[...]


def _matmul_residual(a, w, x3, gate3, tm, tn, tk, name):
    m, k = a.shape
    n = w.shape[-1]
    g, tg, _ = x3.shape
    gb = tm // tg
    nt = m // tm
    te = jnp.zeros((nt,), jnp.int32)
    tv = jnp.ones((nt,), jnp.int32)
    grid_spec = pltpu.PrefetchScalarGridSpec(
        num_scalar_prefetch=2,
        grid=(nt, n // tn, k // tk),
        in_specs=[pl.BlockSpec((tm, tk), lambda i, j, kk, te, tv: (i, kk)),
                  pl.BlockSpec((1, tk, tn), lambda i, j, kk, te, tv: (te[i], kk, j)),
                  pl.BlockSpec((gb, tg, tn), lambda i, j, kk, te, tv: (i, 0, j)),
                  pl.BlockSpec((gb, 1, tn), lambda i, j, kk, te, tv: (i, 0, j))],
        out_specs=pl.BlockSpec((gb, tg, tn), lambda i, j, kk, te, tv: (i, 0, j)),
        scratch_shapes=[pltpu.VMEM((tm, tn), F32)],
    )
    return pl.pallas_call(
        _kt_residual_kernel,
        grid_spec=grid_spec,
        out_shape=jax.ShapeDtypeStruct(x3.shape, F32),
        compiler_params=_params("arbitrary", "arbitrary", "arbitrary"),
        name=name,
    )(te, tv, a, w.reshape((1,) + w.shape[-2:]), x3, gate3)


def _expert_down_call(a, w, rowscale, items, tr, nsub, tn, tk, name):
    m, k = a.shape
    n = w.shape[-1]
    tb = tr * nsub
    nk = k // tk
    n_items = items[0].shape[0]

    def kk_of(it, kk, pf):
        return jnp.where(pf[4][it] != 0, kk, nk - 1)

    grid_spec = pltpu.PrefetchScalarGridSpec(
        num_scalar_prefetch=7,
        grid=(n // tn, n_items, nk),
        in_specs=[pl.BlockSpec((tb, tk), lambda j, it, kk, *pf: (pf[0][it], kk_of(it, kk, pf))),
                  pl.BlockSpec((1, tk, tn), lambda j, it, kk, *pf: (pf[1][it], kk_of(it, kk, pf), j)),
                  pl.BlockSpec((1, tb, 1), lambda j, it, kk, *pf: (pf[0][it], 0, 0))],
        out_specs=pl.BlockSpec((tb, tn), lambda j, it, kk, *pf: (pf[0][it], j)),
        scratch_shapes=[pltpu.VMEM((tb, tn), F32)],
    )
    return pl.pallas_call(
        functools.partial(_expert_down_kernel, nsub=nsub, tr=tr),
        grid_spec=grid_spec,
        out_shape=jax.ShapeDtypeStruct((m, n), F32),
        compiler_params=_params("arbitrary", "arbitrary", "arbitrary"),
        name=name,
    )(*items, a, w, rowscale.reshape(m // tb, tb, 1))


def _conv_kernel(x_ref, prev_ref, w_ref, b_ref, o_ref, xp_ref, *, tl, taps):
    t = pl.program_id(2)
    hist = taps - 1
    top = SUBLANES - hist

    @pl.when(t == 0)
    def _():
        xp_ref[top:SUBLANES, :] = prev_ref[0]

    @pl.when(t > 0)
    def _():
        xp_ref[0:SUBLANES, :] = xp_ref[tl:tl + SUBLANES, :]

    x = x_ref[...]
    xp_ref[SUBLANES:SUBLANES + tl, :] = x
    w = w_ref[...]
    acc = b_ref[...] + w[hist:taps] * x
    for k in range(hist):
        acc = acc + w[k:k + 1] * xp_ref[top + k:top + k + tl, :]
    o_ref[...] = _silu(acc)


def _conv_call(proj, row0, nb, seq, col0, conv_prev, conv_w, conv_b, name):
    taps, c = conv_w.shape
    tl = _tile(seq, 512, SUBLANES)
    tc = _tile(math.gcd(c, col0) if col0 else c, 1024)
    nt = seq // tl
    assert row0 % tl == 0 and col0 % tc == 0
    rb0, cb0 = row0 // tl, col0 // tc
    return pl.pallas_call(
        functools.partial(_conv_kernel, tl=tl, taps=taps),
        grid=(nb, c // tc, nt),
        in_specs=[pl.BlockSpec((tl, tc), lambda b, j, t: (rb0 + b * nt + t, cb0 + j)),
                  pl.BlockSpec((1, taps - 1, tc), lambda b, j, t: (b, 0, j)),
                  pl.BlockSpec((taps, tc), lambda b, j, t: (0, j)),
                  pl.BlockSpec((1, tc), lambda b, j, t: (0, j))],
        out_specs=pl.BlockSpec((tl, tc), lambda b, j, t: (b * nt + t, j)),
        out_shape=jax.ShapeDtypeStruct((nb * seq, c), F32),
        scratch_shapes=[pltpu.VMEM((tl + SUBLANES, tc), F32)],
        compiler_params=_params("arbitrary", "arbitrary", "arbitrary"),
        name=name,
    )(proj, conv_prev, conv_w, conv_b.reshape(1, c))


def _ssd_kernel(x_ref, b_ref, c_ref, dtr_ref, z_ref, sel_ref, e_ref, bias_ref, alog_ref, dskip_ref, nw_ref,
                s0_ref, g_ref, s_ref, st_ref, *, q, heads, hdim, gpb):
    c = pl.program_id(2)
    rp = heads * hdim
    n = b_ref.shape[1] // gpb
    per_lane = LANES // hdim

    @pl.when(c == 0)
    def _():
        for k in range(gpb):
            st_ref[k] = s0_ref[0, k].T

    dt_all = _softplus(dtr_ref[...] + bias_ref[...])
    da_all = dt_all * -jnp.exp(alog_ref[...])
    ri = lax.broadcasted_iota(jnp.int32, (q, q), 0)
    ci = lax.broadcasted_iota(jnp.int32, (q, q), 1)
    causal = ri >= ci
    tril = jnp.where(causal, 1.0, 0.0).astype(BF16)
    expand = e_ref[...]
    lane = lax.broadcasted_iota(jnp.int32, (q, LANES), 1)

    for k in range(gpb):
        cols = slice(k * rp, (k + 1) * rp)
        sel = sel_ref[k]
        dtg = _sel_dot(dt_all, sel)
        dag = _sel_dot(da_all, sel)
        acs = _sel_dot_left(tril, dag)
        acs_t = acs.T
        dt_exp = _sel_dot(dtg, expand)
        acs_exp = _sel_dot(acs, expand)

        x = x_ref[:, cols]
        xdt = x * dt_exp
        b_blk = b_ref[:, k * n:(k + 1) * n]
        bm = b_blk.astype(BF16)
        cm = c_ref[:, k * n:(k + 1) * n].astype(BF16)
        cb = _dot_nt(cm, bm)

        st = st_ref[k]
        y = _dot(cm, st.astype(BF16)) * jnp.exp(acs_exp)
        last = acs_exp[q - 1:q, :]
        xs = (xdt * jnp.exp(last - acs_exp)).astype(BF16)
        st_ref[k] = st * jnp.exp(last) + _dot(b_blk.T.astype(BF16), xs)

        pieces = []
        for pr in range(heads // per_lane):
            xp = xdt[:, pr * LANES:(pr + 1) * LANES]
            acc = None
            for hh in range(per_lane):
                r = pr * per_lane + hh
                seg = acs[:, r:r + 1] - acs_t[r:r + 1, :]
                dec = jnp.exp(jnp.where(causal, seg, -jnp.inf))
                m = (cb * dec).astype(BF16)
                in_head = jnp.logical_and(lane >= hh * hdim, lane < (hh + 1) * hdim)
                xm = jnp.where(in_head, xp, 0.0).astype(BF16)
                t = _dot(m, xm)
                acc = t if acc is None else acc + t
            pieces.append(acc)
        y = y + jnp.concatenate(pieces, axis=1) + dskip_ref[:, cols] * x

        gt = y * _silu(z_ref[:, cols])
        ms = jnp.mean(gt * gt, axis=-1, keepdims=True)
        g_ref[:, cols] = (gt * lax.rsqrt(ms + EPS) * nw_ref[:, cols]).astype(g_ref.dtype)

    @pl.when(c == pl.num_programs(2) - 1)
    def _():
        for k in range(gpb):
            s_ref[0, k] = st_ref[k].T


def _ssd_call(xbc, proj, dt_raw, row0, nb, seq, s0, consts, name):
    sel, expand, bias, alog, dskip, nw = consts
    groups, n_heads, _ = sel.shape
    _, _, rp, n = s0.shape
    heads = n_heads // groups
    hdim = rp // heads
    inner = groups * rp
    gpb = 4 if groups % 4 == 0 else (2 if groups % 2 == 0 else 1)
    q = SSD_CHUNK if seq % SSD_CHUNK == 0 else seq
    nc = seq // q
    assert row0 % q == 0 and LANES % hdim == 0 and heads % (LANES // hdim) == 0 and heads <= LANES
    assert (inner // n) % gpb == 0
    rb0 = row0 // q
    b_col0 = inner // (gpb * n)
    c_col0 = (inner + groups * n) // (gpb * n)
    return pl.pallas_call(
        functools.partial(_ssd_kernel, q=q, heads=heads, hdim=hdim, gpb=gpb),
        grid=(nb, groups // gpb, nc),
        in_specs=[pl.BlockSpec((q, gpb * rp), lambda b, g, c: (b * nc + c, g)),
                  pl.BlockSpec((q, gpb * n), lambda b, g, c: (b * nc + c, b_col0 + g)),
                  pl.BlockSpec((q, gpb * n), lambda b, g, c: (b * nc + c, c_col0 + g)),
                  pl.BlockSpec((q, n_heads), lambda b, g, c: (rb0 + b * nc + c, 0)),
                  pl.BlockSpec((q, gpb * rp), lambda b, g, c: (rb0 + b * nc + c, g)),
                  pl.BlockSpec((gpb, n_heads, LANES), lambda b, g, c: (g, 0, 0)),
                  pl.BlockSpec((LANES, rp), lambda b, g, c: (0, 0)),
                  pl.BlockSpec((1, n_heads), lambda b, g, c: (0, 0)),
                  pl.BlockSpec((1, n_heads), lambda b, g, c: (0, 0)),
                  pl.BlockSpec((1, gpb * rp), lambda b, g, c: (0, g)),
                  pl.BlockSpec((1, gpb * rp), lambda b, g, c: (0, g)),
                  pl.BlockSpec((1, gpb, rp, n), lambda b, g, c: (b, g, 0, 0))],
        out_specs=[pl.BlockSpec((q, gpb * rp), lambda b, g, c: (b * nc + c, g)),
                   pl.BlockSpec((1, gpb, rp, n), lambda b, g, c: (b, g, 0, 0))],
        out_shape=[jax.ShapeDtypeStruct((nb * seq, inner), BF16),
                   jax.ShapeDtypeStruct(s0.shape, F32)],
        scratch_shapes=[pltpu.VMEM((gpb, n, rp), F32)],
        compiler_params=_params("arbitrary", "arbitrary", "arbitrary"),
        name=name,
    )(xbc, xbc, xbc, dt_raw, proj, sel, expand, bias, alog, dskip, nw, s0)


def _sb_weights(z, carry, suffix, strict_mask):
    sub = suffix.shape[0]
    tail = jnp.log(1.0 + jnp.exp(-jnp.abs(z)))
    log_beta = jnp.minimum(z, 0.0) - tail
    log_keep = jnp.minimum(-z, 0.0) - tail
    if strict_mask is not None:
        log_keep = jnp.where(strict_mask, log_keep, 0.0)
    nsub = z.shape[1] // sub
    later = [None] * nsub
    run = carry
    for j in reversed(range(nsub)):
        lk = log_keep[:, j * sub:(j + 1) * sub]
        later[j] = _sel_dot(lk, suffix, parts=2) + run
        run = run + jnp.sum(lk, axis=1, keepdims=True)
    later = later[0] if nsub == 1 else jnp.concatenate(later, axis=1)
    w = jnp.exp(log_beta + later)
    if strict_mask is not None:
        w = jnp.where(strict_mask, w, 0.0)
    return w, run


def _suffix_matrix(n):
    ri = lax.broadcasted_iota(jnp.int32, (n, n), 0)
    ci = lax.broadcasted_iota(jnp.int32, (n, n), 1)
    return jnp.where(ri > ci, 1.0, 0.0).astype(BF16)


def _stacked_strict_mask(nh, t):
    ri = lax.broadcasted_iota(jnp.int32, (nh * t, t), 0)
    ci = lax.broadcasted_iota(jnp.int32, (nh * t, t), 1)
    return ci < jnp.bitwise_and(ri, t - 1)


def _sb_scores(q_ref, load_k, nh, hd, scale, h0=0):
    zs = [_dot_nt(q_ref[:, (h0 + h) * hd:(h0 + h + 1) * hd], load_k(h0 + h)) for h in range(nh)]
    return jnp.concatenate(zs, axis=0) * scale


def _sb_values(w, load_v, nh, t, h0=0):
    wb = w.astype(BF16)
    return jnp.concatenate([_dot(wb[h * t:(h + 1) * t], load_v(h0 + h)) for h in range(nh)], axis=0)


def _sb_prompt_kernel(q_ref, k_ref, v_ref, o_ref, acc_ref, carry_ref, *, blk, nh, hd, scale):
    qi = pl.program_id(2)
    suffix = _suffix_matrix(blk)

    def visit(kb, carry, mask):
        ks = pl.multiple_of(kb * blk, blk)
        z = _sb_scores(q_ref, lambda h: k_ref[pl.ds(ks, blk), h * hd:(h + 1) * hd], nh, hd, scale)
        w, run = _sb_weights(z, carry, suffix, mask)
        return _sb_values(w, lambda h: v_ref[pl.ds(ks, blk), h * hd:(h + 1) * hd], nh, blk), run

    out, run = visit(qi, jnp.zeros((nh * blk, 1), F32), _stacked_strict_mask(nh, blk))
    acc_ref[...] = out
    carry_ref[...] = run

    def body(i, c):
        out, run = visit(qi - 1 - i, carry_ref[...], None)
        acc_ref[...] += out
        carry_ref[...] = run
        return c

    lax.fori_loop(0, qi, body, 0)
    for h in range(nh):
        o_ref[:, h * hd:(h + 1) * hd] = acc_ref[h * blk:(h + 1) * blk, :].astype(o_ref.dtype)


def _heads_per_step(n_heads):
    return SUBLANES if n_heads % SUBLANES == 0 else n_heads


def _sb_prompt_call(q16, kv16, row0, nb, seq, n_heads, hd, scale, name):
    blk = SB_BLOCK
    nq = seq // blk
    nh = _heads_per_step(n_heads)
    ng = n_heads // nh
    assert row0 % seq == 0 and hd % LANES == 0 and blk & (blk - 1) == 0
    sb0 = row0 // seq
    rb0 = row0 // blk
    return pl.pallas_call(
        functools.partial(_sb_prompt_kernel, blk=blk, nh=nh, hd=hd, scale=scale),
        grid=(nb, ng, nq),
        in_specs=[pl.BlockSpec((blk, nh * hd), lambda b, g, i: (rb0 + b * nq + i, g)),
                  pl.BlockSpec((seq, nh * hd), lambda b, g, i: (sb0 + b, g)),
                  pl.BlockSpec((seq, nh * hd), lambda b, g, i: (sb0 + b, ng + g))],
        out_specs=pl.BlockSpec((blk, nh * hd), lambda b, g, i: (b * nq + i, g)),
        out_shape=jax.ShapeDtypeStruct((nb * seq, n_heads * hd), BF16),
        scratch_shapes=[pltpu.VMEM((nh * blk, hd), F32), pltpu.VMEM((nh * blk, 1), F32)],
        compiler_params=_params("arbitrary", "arbitrary", "arbitrary"),
        name=name,
    )(q16, kv16, kv16)


def _sb_decode_kernel(q_ref, kn_ref, vn_ref, kc_ref, vc_ref, o_ref, acc_ref, carry_ref, *, nh, hd, sub, scale):
    p = pl.program_id(2)
    t = q_ref.shape[0]

    @pl.when(p == 0)
    def _():
        z = _sb_scores(q_ref, lambda h: kn_ref[:, h * hd:(h + 1) * hd], nh, hd, scale)
        w, run = _sb_weights(z, jnp.zeros((nh * t, 1), F32), _suffix_matrix(t), _stacked_strict_mask(nh, t))
        acc_ref[...] = _sb_values(w, lambda h: vn_ref[:, h * hd:(h + 1) * hd], nh, t)
        carry_ref[...] = run

    pb = kc_ref.shape[0]
    k_rows = kc_ref.reshape(pb * nh, hd)
    v_rows = vc_ref.reshape(pb * nh, hd)
    z = _sb_scores(q_ref, lambda h: k_rows[pl.ds(h, pb, stride=nh), :].astype(BF16), nh, hd, scale)
    w, run = _sb_weights(z, carry_ref[...], _suffix_matrix(sub), None)
    acc_ref[...] += _sb_values(w, lambda h: v_rows[pl.ds(h, pb, stride=nh), :].astype(BF16), nh, t)
    carry_ref[...] = run

    @pl.when(p == pl.num_programs(2) - 1)
    def _():
        for h in range(nh):
            o_ref[:, h * hd:(h + 1) * hd] = acc_ref[h * t:(h + 1) * t, :].astype(o_ref.dtype)


def _sb_decode_call(q16, kv16, row0, nb, seq, n_heads, hd, cache_k, cache_v, scale, name):
    past = cache_k.shape[1]
    nh = _heads_per_step(n_heads)
    ng = n_heads // nh
    pb = _tile(past, 512)
    sub = _tile(pb, 256)
    npb = past // pb
    assert row0 % seq == 0 and seq & (seq - 1) == 0
    sb0 = row0 // seq
    return pl.pallas_call(
        functools.partial(_sb_decode_kernel, nh=nh, hd=hd, sub=sub, scale=scale),
        grid=(nb, ng, npb),
        in_specs=[pl.BlockSpec((seq, nh * hd), lambda b, g, p: (sb0 + b, g)),
                  pl.BlockSpec((seq, nh * hd), lambda b, g, p: (sb0 + b, g)),
                  pl.BlockSpec((seq, nh * hd), lambda b, g, p: (sb0 + b, ng + g)),
                  pl.BlockSpec((None, pb, nh, hd), lambda b, g, p: (b, npb - 1 - p, g, 0)),
                  pl.BlockSpec((None, pb, nh, hd), lambda b, g, p: (b, npb - 1 - p, g, 0))],
        out_specs=pl.BlockSpec((seq, nh * hd), lambda b, g, p: (b, g)),
        out_shape=jax.ShapeDtypeStruct((nb * seq, n_heads * hd), BF16),
        scratch_shapes=[pltpu.VMEM((nh * seq, hd), F32), pltpu.VMEM((nh * seq, 1), F32)],
        compiler_params=_params("arbitrary", "arbitrary", "arbitrary"),
        name=name,
    )(q16, kv16, kv16, cache_k, cache_v)


def _gather_cast_kernel(src_ref, h_ref, o_ref, buf_ref, sem):
    tm = buf_ref.shape[0]

    def row_copy(r):
        return pltpu.make_async_copy(h_ref.at[pl.ds(src_ref[0, 0, r], 1), :], buf_ref.at[pl.ds(r, 1), :], sem)

    def start(r2, carry):
        for thread in range(2):
            row_copy(2 * r2 + thread).start(priority=thread)
        return carry

    def wait(r, carry):
        row_copy(r).wait()
        return carry

    lax.fori_loop(0, tm // 2, start, 0)
    lax.fori_loop(0, tm, wait, 0)
    o_ref[...] = buf_ref[...].astype(o_ref.dtype)


def _gather_cast_call(h, src_rows, tm):
    m, d = h.shape
    nt = src_rows.shape[0] // tm
    return pl.pallas_call(
        _gather_cast_kernel,
        grid=(nt,),
        in_specs=[pl.BlockSpec((1, 1, tm), lambda i: (i, 0, 0), memory_space=pltpu.SMEM),
                  pl.BlockSpec(memory_space=pl.ANY)],
        out_specs=pl.BlockSpec((tm, d), lambda i: (i, 0)),
        out_shape=jax.ShapeDtypeStruct((nt * tm, d), BF16),
        scratch_shapes=[pltpu.VMEM((tm, d), F32), pltpu.SemaphoreType.DMA(())],
        compiler_params=_params("arbitrary"),
        name="moe_gather_rows",
    )(src_rows.reshape(nt, 1, tm), h)


def _combine_kernel(p1_ref, p2_ref, ys_ref, x_ref, g_ref, o_ref, buf_ref, sem):
    tm = buf_ref.shape[1]

    def copies(r):
        return (pltpu.make_async_copy(ys_ref.at[pl.ds(p1_ref[0, 0, r], 1), :], buf_ref.at[0, pl.ds(r, 1), :], sem),
                pltpu.make_async_copy(ys_ref.at[pl.ds(p2_ref[0, 0, r], 1), :], buf_ref.at[1, pl.ds(r, 1), :], sem))

    def start(r, carry):
        for thread, cp in enumerate(copies(r)):
            cp.start(priority=thread)
        return carry

    def wait(r, carry):
        for cp in copies(r):
            cp.wait()
        return carry

    lax.fori_loop(0, tm, start, 0)
    lax.fori_loop(0, tm, wait, 0)
    f = (buf_ref[0] + buf_ref[1]).reshape(o_ref.shape)
    o_ref[...] = x_ref[...] + g_ref[...] * f


def _combine_call(ys, pos1, pos2, x3, gate3, tm):
    g, tg, d = x3.shape
    gb = tm // tg
    nt = (g * tg) // tm
    return pl.pallas_call(
        _combine_kernel,
        grid=(nt,),
        in_specs=[pl.BlockSpec((1, 1, tm), lambda i: (i, 0, 0), memory_space=pltpu.SMEM),
                  pl.BlockSpec((1, 1, tm), lambda i: (i, 0, 0), memory_space=pltpu.SMEM),
                  pl.BlockSpec(memory_space=pl.ANY),
                  pl.BlockSpec((gb, tg, d), lambda i: (i, 0, 0)),
                  pl.BlockSpec((gb, 1, d), lambda i: (i, 0, 0))],
        out_specs=pl.BlockSpec((gb, tg, d), lambda i: (i, 0, 0)),
        out_shape=jax.ShapeDtypeStruct(x3.shape, F32),
        scratch_shapes=[pltpu.VMEM((2, tm, d), F32), pltpu.SemaphoreType.DMA(())],
        compiler_params=_params("arbitrary"),
        name="moe_combine_rows",
    )(pos1.reshape(nt, 1, tm), pos2.reshape(nt, 1, tm), ys, x3, gate3)


def _final_norm_kernel(x_ref, w_ref, o_ref):
    x = x_ref[...]
    ms = jnp.mean(x * x, axis=-1, keepdims=True)
    o_ref[...] = x * lax.rsqrt(ms + EPS) * w_ref[...]


def _final_norm_call(x2, w, row0, n_rows, name):
    _, d = x2.shape
    tm = _tile(math.gcd(n_rows, row0) if row0 else n_rows, 256, SUBLANES)
    rb0 = row0 // tm
    return pl.pallas_call(
        _final_norm_kernel,
        grid=(n_rows // tm,),
        in_specs=[pl.BlockSpec((tm, d), lambda i: (rb0 + i, 0)), pl.BlockSpec((1, d), lambda i: (0, 0))],
        out_specs=pl.BlockSpec((tm, d), lambda i: (i, 0)),
        out_shape=jax.ShapeDtypeStruct((n_rows, d), F32),
        compiler_params=_params("arbitrary"),
        name=name,
    )(x2, w.reshape(1, d))


def _ssd_constants(groups, n_heads, hdim, dt_bias, a_log, d_skip, norm_w):
    heads = n_heads // groups
    sel = np.zeros((groups, n_heads, LANES), np.float32)
    for g in range(groups):
        for r in range(heads):
            sel[g, g * heads + r, r] = 1.0
    expand = np.zeros((LANES, heads * hdim), np.float32)
    for r in range(heads):
        expand[r, r * hdim:(r + 1) * hdim] = 1.0
    return (jnp.asarray(sel, BF16), jnp.asarray(expand, BF16),
            dt_bias.astype(F32).reshape(1, n_heads), a_log.astype(F32).reshape(1, n_heads),
            jnp.repeat(d_skip.astype(F32), hdim).reshape(1, n_heads * hdim), norm_w.reshape(1, n_heads * hdim))


def _route(route, n_experts, tr, nsub):
    m = route.shape[0]
    idx = route[:, :TOP_K].astype(jnp.int32)
    gates = route[:, TOP_K:2 * TOP_K]
    flat_e = idx.reshape(-1)
    n_assign = flat_e.shape[0]
    n_tiles = -(-(n_assign // tr + n_experts) // nsub) * nsub
    n_blocks = n_tiles // nsub
    order = jnp.argsort(flat_e, stable=True)
    rank = jnp.argsort(order).astype(jnp.int32)
    counts = jnp.sum((flat_e[:, None] == jnp.arange(n_experts, dtype=jnp.int32)[None, :]).astype(jnp.int32), axis=0)
    tiles = (counts + tr - 1) // tr
    padded = tiles * tr
    pad_end = jnp.cumsum(padded)
    pad_start = pad_end - padded
    cnt_start = jnp.cumsum(counts) - counts
    dest2 = (pad_start[flat_e] + rank - cnt_start[flat_e]).reshape(m, TOP_K)

    t_idx = jnp.arange(n_tiles, dtype=jnp.int32)
    t_start = t_idx * tr
    sub_e = jnp.minimum(jnp.searchsorted(pad_end, t_start, side="right"), n_experts - 1).astype(jnp.int32)
    sub_v = t_start < pad_end[-1]
    row = jnp.arange(n_tiles * tr, dtype=jnp.int32)
    row_e = jnp.repeat(sub_e, tr)
    within = row - pad_start[row_e]
    holds = within < counts[row_e]
    assign = order[jnp.clip(cnt_start[row_e] + within, 0, n_assign - 1)]
    src_rows = jnp.where(holds, assign // TOP_K, 0).astype(jnp.int32)
    row_gate = jnp.where(holds, gates.reshape(-1)[assign], 0.0)
    prev_e = jnp.concatenate([jnp.full((1,), -1, jnp.int32), sub_e[:-1]])
    run_start = jnp.logical_or(t_idx % nsub == 0, jnp.logical_and(sub_v, sub_e != prev_e))
    n_items = n_blocks + n_experts - 1
    item_sub = jnp.nonzero(run_start, size=n_items, fill_value=-1)[0].astype(jnp.int32)
    item_exists = item_sub >= 0
    n_exist = jnp.sum(item_exists.astype(jnp.int32))
    item_sub = jnp.where(item_exists, item_sub, item_sub[jnp.maximum(n_exist - 1, 0)])
    item_live = jnp.logical_and(item_exists, sub_v[item_sub])
    item_blk = item_sub // nsub
    item_exp = sub_e[item_sub]
    item_first = jnp.logical_and(item_exists, item_sub % nsub == 0)
    next_blk = jnp.concatenate([item_blk[1:], jnp.full((1,), -1, jnp.int32)])
    next_exists = jnp.concatenate([item_exists[1:], jnp.zeros((1,), bool)])
    item_last = jnp.logical_and(item_exists, jnp.logical_or(jnp.logical_not(next_exists), next_blk != item_blk))

    def i32(v):
        return v.astype(jnp.int32)

    items = (i32(item_blk), i32(item_exp), i32(item_first), i32(item_last), i32(item_live), sub_e, i32(sub_v))
    return src_rows, row_gate, i32(pad_start), i32(tiles), items, dest2[:, 0], dest2[:, 1]


def kernel(x_prompt, x_sample, state_ssm, state_conv, cache_k, cache_v, c_prompt, c_sample, w_mod, b_mod, ssd_w_in, ssd_conv_w, ssd_conv_b, ssd_dt_bias, ssd_a_log, ssd_d, ssd_norm_w, ssd_w_out, sb_w_qkv, sb_w_o, ffn_w_gate, ffn_w_up, ffn_w_down, moe_w_router, moe_w_gate, moe_w_up, moe_w_down, final_norm_w):
    bp, lp, d = x_prompt.shape
    bs, ls, _ = x_sample.shape
    depth = w_mod.shape[0]
    mp, msamp = bp * lp, bs * ls
    m = mp + msamp
    tg = math.gcd(lp, ls)
    assert tg % SUBLANES == 0
    n_groups = m // tg
    n_heads = ssd_a_log.shape[1]
    hdim, n_state = state_ssm.shape[3], state_ssm.shape[4]
    inner = n_heads * hdim
    conv_dim = ssd_conv_w.shape[2]
    ssd_groups = (conv_dim - inner) // (2 * n_state)
    sb_heads, sb_hd = cache_k.shape[3], cache_k.shape[4]
    n_experts = moe_w_router.shape[2]
    d_ff = ffn_w_gate.shape[2]
    tm = _tile(m, 1024, tg)
    tm_res = _tile(m, 1536, tg)

    seq_of_group = np.concatenate([np.repeat(np.arange(bp), lp // tg), bp + np.repeat(np.arange(bs), ls // tg)])
    n_seq = bp + bs
    rows = -(-n_seq // 16) * 16
    c_all = jnp.zeros((rows, d), F32).at[:n_seq].set(jnp.concatenate([c_prompt, c_sample], axis=0))
    mod = _mod_call(c_all, w_mod, b_mod)
    mod_g = mod[:, seq_of_group, :].reshape(depth, n_groups, 1, 6, d)

    def mvec(i, which):
        return mod_g[i, :, :, which, :]

    x3 = jnp.concatenate([x_prompt.reshape(mp, d), x_sample.reshape(msamp, d)], axis=0).reshape(n_groups, tg, d)
    zeros_ssm = jnp.zeros((bp,) + state_ssm.shape[2:], F32)
    zeros_conv = jnp.zeros((bp,) + state_conv.shape[2:], F32)
    ssm_p, conv_p, k_p, v_p, ssm_s, conv_s, k_s, v_s = [], [], [], [], [], [], [], []

    for i in range(depth):
        j = i // 2
        h = _modulate_call(x3, mvec(i, 0), mvec(i, 1), BF16).reshape(m, d)
        if i % 2 == 0:
            w_in = ssd_w_in[j]
            n_main = inner + conv_dim
            proj, = _matmul_ws(h, w_in, 0, n_main, _tile(n_main, 512), tm, "ssd_in_proj")
            dt_raw, = _matmul_ws(h, w_in, n_main, n_heads, n_heads, tm, "ssd_in_proj_dt")
            consts = _ssd_constants(ssd_groups, n_heads, hdim, ssd_dt_bias[j], ssd_a_log[j], ssd_d[j], ssd_norm_w[j])
            outs = []
            for (row0, nb, seq, prev, s0, conv_out, ssm_out, tag) in (
                    (0, bp, lp, zeros_conv, zeros_ssm, conv_p, ssm_p, "prompt"),
                    (mp, bs, ls, state_conv[j], state_ssm[j], conv_s, ssm_s, "sample")):
                xbc = _conv_call(proj, row0, nb, seq, inner, prev, ssd_conv_w[j], ssd_conv_b[j], "ssd_conv_" + tag)
                s0g = s0.astype(F32).reshape(nb, ssd_groups, (n_heads // ssd_groups) * hdim, n_state)
                g_out, s_fin = _ssd_call(xbc, proj, dt_raw, row0, nb, seq, s0g, consts, "ssd_scan_" + tag)
                outs.append(g_out)
                ssm_out.append(s_fin.reshape(nb, n_heads, hdim, n_state))
                hist = ssd_conv_w.shape[1] - 1
                last_rows = (row0 + np.arange(nb)[:, None] * seq + np.arange(seq - hist, seq)[None, :]).reshape(-1)
                conv_out.append(jnp.take(proj, last_rows, axis=0)[:, inner:].reshape(nb, hist, conv_dim))
            mix = jnp.concatenate(outs, axis=0)
            w_mix = ssd_w_out[j]
        else:
            hd_all = sb_heads * sb_hd
            tn = _tile(hd_all, 512)
            q16, = _matmul_ws(h, sb_w_qkv[j], 0, hd_all, tn, tm, "sb_q", (BF16,))
            kv32, kv16 = _matmul_ws(h, sb_w_qkv[j], hd_all, 2 * hd_all, tn, tm, "sb_kv", (F32, BF16))
            scale = float(sb_hd) ** -0.5
            o_p = _sb_prompt_call(q16, kv16, 0, bp, lp, sb_heads, sb_hd, scale, "sb_attn_prompt")
            o_s = _sb_decode_call(q16, kv16, mp, bs, ls, sb_heads, sb_hd, cache_k[j], cache_v[j], scale,
                                  "sb_attn_sample")
            mix = jnp.concatenate([o_p, o_s], axis=0)
            w_mix = sb_w_o[j]
            for (row0, nb, seq, k_out, v_out) in ((0, bp, lp, k_p, v_p), (mp, bs, ls, k_s, v_s)):
                blk = kv32[row0:row0 + nb * seq]
                k_out.append(blk[:, :hd_all].reshape(nb, seq, sb_heads, sb_hd))
                v_out.append(blk[:, hd_all:].reshape(nb, seq, sb_heads, sb_hd))
        kmix = mix.shape[1]
        x3 = _matmul_residual(mix, w_mix, x3, mvec(i, 2), tm_res, _tile(d, 1024), _tile(kmix, 1024), "mixer_out_proj")

        if i % 2 == 0:
            h = _modulate_call(x3, mvec(i, 3), mvec(i, 4), BF16).reshape(m, d)
            nt = m // tm
            act = _swiglu_call(h, ffn_w_gate[j][None], ffn_w_up[j][None], jnp.zeros((nt,), jnp.int32),
                               jnp.ones((nt,), jnp.int32), tm, _tile(d_ff, 256), "ffn_up")
            x3 = _matmul_residual(act, ffn_w_down[j], x3, mvec(i, 5), tm_res, _tile(d, 1024), _tile(d_ff, 1024), "ffn_down")
        else:
            h32, route = _modulate_router_call(x3, mvec(i, 3), mvec(i, 4), moe_w_router[j])
            tr = _tile(m, 256, tg)
            nsub = 4
            src_rows, row_gate, group_start, group_tiles, items, pos1, pos2 = _route(
                route.reshape(m, LANES), n_experts, tr, nsub)
            hs = _gather_cast_call(h32.reshape(m, d), src_rows, nsub * tr // 2)
            act = _expert_up_call(hs, moe_w_gate[j], moe_w_up[j], group_start, group_tiles, tr, _tile(d_ff, 512),
                                  "moe_up")
            ys = _expert_down_call(act, moe_w_down[j], row_gate, items, tr, nsub, _tile(d, 2048), _tile(d_ff, 1024),
                                   "moe_down")
            x3 = _combine_call(ys, pos1, pos2, x3, mvec(i, 5), _tile(m, 256, tg))

    x2 = x3.reshape(m, d)
    y_prompt = _final_norm_call(x2, final_norm_w, 0, mp, "final_norm_prompt").reshape(bp, lp, d)
    y_sample = _final_norm_call(x2, final_norm_w, mp, msamp, "final_norm_sample").reshape(bs, ls, d)
    return (y_prompt, y_sample, jnp.stack(ssm_p), jnp.stack(conv_p), jnp.stack(k_p), jnp.stack(v_p),
            jnp.stack(ssm_s), jnp.stack(conv_s), jnp.stack(k_s), jnp.stack(v_s))
```

```python
import functools
import math

import numpy as np
import jax
import jax.numpy as jnp
from jax import lax
from jax.experimental import pallas as pl
from jax.experimental.pallas import tpu as pltpu

F32 = jnp.float32
BF16 = jnp.bfloat16
EPS = 1e-6
SSD_CHUNK = 64
SB_BLOCK = 128
TOP_K = 2
LANES = 128
SUBLANES = 8
VMEM_LIMIT = 56 * 1024 * 1024
ROW_DMA_THREAD = 1


def _tile(n, pref, align=LANES):
    t = min(pref, n)
    t -= t % align
    while t >= align:
        if n % t == 0:
            return t
        t -= align
    return n


def _params(*sem):
    return pltpu.CompilerParams(dimension_semantics=sem, vmem_limit_bytes=VMEM_LIMIT)


def _sigmoid(x):
    return 1.0 / (1.0 + jnp.exp(-x))


def _silu(x):
    return x * _sigmoid(x)


def _softplus(x):
    return jnp.maximum(x, 0.0) + jnp.log1p(jnp.exp(-jnp.abs(x)))


def _split_bf16(x, parts):
    out = []
    r = x
    for _ in range(parts - 1):
        p = r.astype(BF16)
        out.append(p)
        r = r - p.astype(F32)
    out.append(r.astype(BF16))
    return out


def _dot(a, b):
    return jnp.dot(a, b, preferred_element_type=F32)


def _dot_nt(a, b):
    return lax.dot_general(a, b, (((1,), (1,)), ((), ())), preferred_element_type=F32)


def _sel_dot(x, onehot, parts=3):
    acc = None
    for p in _split_bf16(x, parts):
        t = _dot(p, onehot)
        acc = t if acc is None else acc + t
    return acc


def _sel_dot_left(onehot, x, parts=3):
    acc = None
    for p in _split_bf16(x, parts):
        t = _dot(onehot, p)
        acc = t if acc is None else acc + t
    return acc


def _mod_kernel(c_ref, w_ref, b_ref, o_ref):
    cs = _silu(c_ref[...]).astype(BF16)
    o_ref[0] = _dot(cs, w_ref[0].astype(BF16)) + b_ref[0]


def _mod_call(c_pad, w_mod, b_mod):
    depth, d, n = w_mod.shape
    rows = c_pad.shape[0]
    tn = _tile(n, 512)
    return pl.pallas_call(
        _mod_kernel,
        grid=(depth, n // tn),
        in_specs=[pl.BlockSpec((rows, d), lambda i, j: (0, 0)),
                  pl.BlockSpec((1, d, tn), lambda i, j: (i, 0, j)),
                  pl.BlockSpec((1, 1, tn), lambda i, j: (i, 0, j))],
        out_specs=pl.BlockSpec((1, rows, tn), lambda i, j: (i, 0, j)),
        out_shape=jax.ShapeDtypeStruct((depth, rows, n), F32),
        compiler_params=_params("arbitrary", "arbitrary"),
        name="adaln_mod",
    )(c_pad, w_mod, b_mod.reshape(depth, 1, n))


def _modulate_kernel(x_ref, sh_ref, sc_ref, o_ref):
    x = x_ref[...]
    ms = jnp.mean(x * x, axis=-1, keepdims=True)
    h = x * lax.rsqrt(ms + EPS) * (1.0 + sc_ref[...]) + sh_ref[...]
    o_ref[...] = h.astype(o_ref.dtype)


def _modulate_call(x3, sh, sc, out_dtype):
    g, tg, d = x3.shape
    gb = _tile(g, max(1, 256 // tg), 1)
    return pl.pallas_call(
        _modulate_kernel,
        grid=(g // gb,),
        in_specs=[pl.BlockSpec((gb, tg, d), lambda i: (i, 0, 0)),
                  pl.BlockSpec((gb, 1, d), lambda i: (i, 0, 0)),
                  pl.BlockSpec((gb, 1, d), lambda i: (i, 0, 0))],
        out_specs=pl.BlockSpec((gb, tg, d), lambda i: (i, 0, 0)),
        out_shape=jax.ShapeDtypeStruct((g, tg, d), out_dtype),
        compiler_params=_params("arbitrary"),
        name="adaln_modulate",
    )(x3, sh, sc)


def _modulate_router_kernel(x_ref, sh_ref, sc_ref, wr_ref, o_ref, r_ref, *, n_experts):
    x = x_ref[...]
    gb, tg, d = x.shape
    ms = jnp.mean(x * x, axis=-1, keepdims=True)
    h = x * lax.rsqrt(ms + EPS) * (1.0 + sc_ref[...]) + sh_ref[...]
    o_ref[...] = h
    h2 = h.reshape(gb * tg, d)
    h_hi, h_lo = _split_bf16(h2, 2)
    w_hi, w_lo = _split_bf16(wr_ref[...], 2)
    logits = _dot(h_hi, w_hi) + _dot(h_hi, w_lo) + _dot(h_lo, w_hi)
    lane = lax.broadcasted_iota(jnp.int32, logits.shape, 1).astype(F32)
    neg = jnp.float32(-jnp.inf)
    lg = jnp.where(lane < n_experts, logits, neg)
    m1 = jnp.max(lg, axis=1, keepdims=True)
    i1 = jnp.min(jnp.where(lg == m1, lane, float(LANES)), axis=1, keepdims=True)
    lg2 = jnp.where(lane == i1, neg, lg)
    m2 = jnp.max(lg2, axis=1, keepdims=True)
    i2 = jnp.min(jnp.where(lg2 == m2, lane, float(LANES)), axis=1, keepdims=True)
    e2 = jnp.exp(m2 - m1)
    g1 = 1.0 / (1.0 + e2)
    g2 = e2 / (1.0 + e2)
    out = jnp.where(lane == 0.0, i1,
                    jnp.where(lane == 1.0, i2,
                              jnp.where(lane == 2.0, g1, jnp.where(lane == 3.0, g2, 0.0))))
    r_ref[...] = out.reshape(gb, tg, LANES)


def _modulate_router_call(x3, sh, sc, w_router):
    g, tg, d = x3.shape
    n_experts = w_router.shape[1]
    wr = jnp.zeros((d, LANES), F32).at[:, :n_experts].set(w_router)
    gb = _tile(g, max(1, 256 // tg), 1)
    return pl.pallas_call(
        functools.partial(_modulate_router_kernel, n_experts=n_experts),
        grid=(g // gb,),
        in_specs=[pl.BlockSpec((gb, tg, d), lambda i: (i, 0, 0)),
                  pl.BlockSpec((gb, 1, d), lambda i: (i, 0, 0)),
                  pl.BlockSpec((gb, 1, d), lambda i: (i, 0, 0)),
                  pl.BlockSpec((d, LANES), lambda i: (0, 0))],
        out_specs=[pl.BlockSpec((gb, tg, d), lambda i: (i, 0, 0)),
                   pl.BlockSpec((gb, tg, LANES), lambda i: (i, 0, 0))],
        out_shape=[jax.ShapeDtypeStruct((g, tg, d), F32),
                   jax.ShapeDtypeStruct((g, tg, LANES), F32)],
        compiler_params=_params("arbitrary"),
        name="adaln_modulate_router",
    )(x3, sh, sc, wr)


def _ws_kernel(a_ref, w_ref, *rest):
    o_refs, wb_ref = rest[:-1], rest[-1]

    @pl.when(pl.program_id(1) == 0)
    def _():
        wb_ref[...] = w_ref[...].astype(BF16)

    acc = _dot(a_ref[...], wb_ref[...])
    for o_ref in o_refs:
        o_ref[...] = acc.astype(o_ref.dtype)


def _matmul_ws(a, w, col_off, n_cols, tn, tm, name, out_dtypes=(F32,)):
    m, k = a.shape
    off = col_off // tn
    assert col_off % tn == 0 and n_cols % tn == 0 and m % tm == 0
    return pl.pallas_call(
        _ws_kernel,
        grid=(n_cols // tn, m // tm),
        in_specs=[pl.BlockSpec((tm, k), lambda n, i: (i, 0)),
                  pl.BlockSpec((k, tn), lambda n, i: (0, n + off))],
        out_specs=[pl.BlockSpec((tm, tn), lambda n, i: (i, n)) for _ in out_dtypes],
        out_shape=[jax.ShapeDtypeStruct((m, n_cols), dt) for dt in out_dtypes],
        scratch_shapes=[pltpu.VMEM((k, tn), BF16)],
        compiler_params=_params("arbitrary", "arbitrary"),
        name=name,
    )(a, w)


def _swiglu_kernel(te_ref, tv_ref, a_ref, wg_ref, wu_ref, o_ref, wgb_ref, wub_ref):
    i = pl.program_id(1)
    fresh = jnp.logical_or(i == 0, te_ref[i] != te_ref[jnp.maximum(i - 1, 0)])

    @pl.when(fresh)
    def _():
        wgb_ref[...] = wg_ref[0].astype(BF16)
        wub_ref[...] = wu_ref[0].astype(BF16)

    @pl.when(tv_ref[i] != 0)
    def _():
        a = a_ref[...]
        g = _dot(a, wgb_ref[...])
        u = _dot(a, wub_ref[...])
        o_ref[...] = (_silu(g) * u).astype(o_ref.dtype)

    @pl.when(tv_ref[i] == 0)
    def _():
        o_ref[...] = jnp.zeros_like(o_ref)


def _swiglu_call(a, w_gate, w_up, tile_expert, tile_valid, tm, tf, name):
    m, k = a.shape
    _, _, f = w_gate.shape
    nt = m // tm
    grid_spec = pltpu.PrefetchScalarGridSpec(
        num_scalar_prefetch=2,
        grid=(f // tf, nt),
        in_specs=[pl.BlockSpec((tm, k), lambda j, i, te, tv: (i, 0)),
                  pl.BlockSpec((1, k, tf), lambda j, i, te, tv: (te[i], 0, j)),
                  pl.BlockSpec((1, k, tf), lambda j, i, te, tv: (te[i], 0, j))],
        out_specs=pl.BlockSpec((tm, tf), lambda j, i, te, tv: (i, j)),
        scratch_shapes=[pltpu.VMEM((k, tf), BF16), pltpu.VMEM((k, tf), BF16)],
    )
    return pl.pallas_call(
        _swiglu_kernel,
        grid_spec=grid_spec,
        out_shape=jax.ShapeDtypeStruct((m, f), BF16),
        compiler_params=_params("arbitrary", "arbitrary"),
        name=name,
    )(tile_expert, tile_valid, a, w_gate, w_up)


def _expert_up_kernel(start_ref, count_ref, hs_ref, wg_ref, wu_ref, act_ref,
                      wgb_ref, wub_ref, a_buf, o_buf, zero_buf, a_sem, o_sem, z_sem, *, tr, tf, n_tiles_total):
    j, e = pl.program_id(0), pl.program_id(1)
    n = count_ref[e]
    row0 = start_ref[e]
    col0 = pl.multiple_of(j * tf, tf)
    full = 2 * tr
    n_full = n // 2
    odd = n - 2 * n_full
    n_chunks = n_full + odd

    def rows(c, size, base=None):
        return pl.ds(pl.multiple_of((row0 if base is None else base) + c * full, tr), size)

    def a_copy(c, slot, size, base=None):
        return pltpu.make_async_copy(hs_ref.at[rows(c, size, base), :], a_buf.at[slot, pl.ds(0, size), :],
                                     a_sem.at[slot])

    def o_copy(c, slot, size):
        return pltpu.make_async_copy(o_buf.at[slot, pl.ds(0, size), :], act_ref.at[rows(c, size), pl.ds(col0, tf)],
                                     o_sem.at[slot])

    def fetch(c, slot, base=None, tiles=None):
        nf = n_full if tiles is None else tiles // 2
        last_half = (odd if tiles is None else tiles - 2 * nf) == 1

        @pl.when(c < nf)
        def _():
            a_copy(c, slot, full, base).start(priority=ROW_DMA_THREAD)

        @pl.when(jnp.logical_and(c == nf, last_half))
        def _():
            a_copy(c, slot, tr, base).start(priority=ROW_DMA_THREAD)

    def consume(c, slot, size):
        a_copy(c, slot, size).wait()
        fetch(c + 1, 1 - slot)
        a = a_buf[slot, pl.ds(0, size), :]
        g = _dot(a, wgb_ref[...])
        u = _dot(a, wub_ref[...])

        @pl.when(c >= 2)
        def _():
            o_copy(c - 2, slot, full).wait()

        o_buf[slot, pl.ds(0, size), :] = (_silu(g) * u).astype(o_buf.dtype)
        o_copy(c, slot, size).start(priority=ROW_DMA_THREAD)

    @pl.when(jnp.logical_and(j == 0, e == 0))
    def _():
        fetch(0, 0)

    wgb_ref[...] = wg_ref[0].astype(BF16)
    wub_ref[...] = wu_ref[0].astype(BF16)

    def body(c, carry):
        consume(c, lax.rem(c, 2), full)
        return carry

    lax.fori_loop(0, n_full, body, 0)

    @pl.when(odd == 1)
    def _():
        consume(n_full, lax.rem(n_full, 2), tr)

    @pl.when(n_chunks >= 2)
    def _():
        o_copy(n_chunks - 2, lax.rem(n_chunks, 2), full).wait()

    @pl.when(jnp.logical_and(n_chunks >= 1, odd == 1))
    def _():
        o_copy(n_chunks - 1, lax.rem(n_chunks + 1, 2), tr).wait()

    @pl.when(jnp.logical_and(n_chunks >= 1, odd == 0))
    def _():
        o_copy(n_chunks - 1, lax.rem(n_chunks + 1, 2), full).wait()

    is_last_step = jnp.logical_and(j == pl.num_programs(0) - 1, e == pl.num_programs(1) - 1)

    @pl.when(jnp.logical_not(is_last_step))
    def _():
        e_next = jnp.where(e == pl.num_programs(1) - 1, 0, e + 1)
        fetch(0, 0, base=start_ref[e_next], tiles=count_ref[e_next])

    @pl.when(e == pl.num_programs(1) - 1)
    def _():
        first = (row0 + n * tr) // tr
        zero_buf[...] = jnp.zeros_like(zero_buf)

        def z_copy(t):
            dst = act_ref.at[pl.ds(pl.multiple_of(t * tr, tr), tr), pl.ds(col0, tf)]
            return pltpu.make_async_copy(zero_buf, dst, z_sem)

        def z_start(t, carry):
            z_copy(t).start()
            return carry

        def z_wait(t, carry):
            z_copy(t).wait()
            return carry

        lax.fori_loop(first, n_tiles_total, z_start, 0)
        lax.fori_loop(first, n_tiles_total, z_wait, 0)


def _expert_up_call(hs, w_gate, w_up, group_start, group_tiles, tr, tf, name):
    m, k = hs.shape
    n_experts, _, f = w_gate.shape
    grid_spec = pltpu.PrefetchScalarGridSpec(
        num_scalar_prefetch=2,
        grid=(f // tf, n_experts),
        in_specs=[pl.BlockSpec(memory_space=pl.ANY),
                  pl.BlockSpec((1, k, tf), lambda j, e, st, ct: (e, 0, j)),
                  pl.BlockSpec((1, k, tf), lambda j, e, st, ct: (e, 0, j))],
        out_specs=pl.BlockSpec(memory_space=pl.ANY),
        scratch_shapes=[pltpu.VMEM((k, tf), BF16), pltpu.VMEM((k, tf), BF16),
                        pltpu.VMEM((2, 2 * tr, k), BF16), pltpu.VMEM((2, 2 * tr, tf), BF16), pltpu.VMEM((tr, tf), BF16),
                        pltpu.SemaphoreType.DMA((2,)), pltpu.SemaphoreType.DMA((2,)), pltpu.SemaphoreType.DMA(())],
    )
    return pl.pallas_call(
        functools.partial(_expert_up_kernel, tr=tr, tf=tf, n_tiles_total=m // tr),
        grid_spec=grid_spec,
        out_shape=jax.ShapeDtypeStruct((m, f), BF16),
        compiler_params=_params("arbitrary", "arbitrary"),
        name=name,
    )(group_start, group_tiles, hs, w_gate, w_up)


def _kt_accumulate(tv_ref, a_ref, w_ref, acc_ref):
    k = pl.program_id(2)

    @pl.when(k == 0)
    def _():
        acc_ref[...] = jnp.zeros_like(acc_ref)

    @pl.when(tv_ref[pl.program_id(0)] != 0)
    def _():
        acc_ref[...] += _dot(a_ref[...], w_ref[0].astype(BF16))


def _kt_residual_kernel(te_ref, tv_ref, a_ref, w_ref, x_ref, g_ref, o_ref, acc_ref):
    _kt_accumulate(tv_ref, a_ref, w_ref, acc_ref)

    @pl.when(pl.program_id(2) == pl.num_programs(2) - 1)
    def _():
        o_ref[...] = x_ref[...] + g_ref[...] * acc_ref[...].reshape(o_ref.shape)


def _expert_down_kernel(iblk_ref, iexp_ref, ifirst_ref, ilast_ref, ivalid_ref, sube_ref, subv_ref,
                        a_ref, w_ref, rs_ref, o_ref, acc_ref, *, nsub, tr):
    it, k = pl.program_id(1), pl.program_id(2)
    blk = iblk_ref[it]
    e = iexp_ref[it]
    live = ivalid_ref[it] != 0

    @pl.when(jnp.logical_and(k == 0, ifirst_ref[it] != 0))
    def _():
        acc_ref[...] = jnp.zeros_like(acc_ref)

    mine = [jnp.logical_and(subv_ref[blk * nsub + s] != 0, sube_ref[blk * nsub + s] == e) for s in range(nsub)]
    whole = functools.reduce(jnp.logical_and, mine)

    @pl.when(jnp.logical_and(live, whole))
    def _():
        acc_ref[...] += _dot(a_ref[...], w_ref[0].astype(BF16))

    for s in range(nsub):
        @pl.when(jnp.logical_and(live, jnp.logical_and(jnp.logical_not(whole), mine[s])))
        def _(s=s):
            acc_ref[s * tr:(s + 1) * tr, :] += _dot(a_ref[s * tr:(s + 1) * tr, :], w_ref[0].astype(BF16))

    @pl.when(jnp.logical_and(k == pl.num_programs(2) - 1, ilast_ref[it] != 0))
    def _():
        o_ref[...] = rs_ref[0] * acc_ref[...]


def _matmul_residual(a, w, x3, gate3, tm, tn, tk, name):
    m, k = a.shape
    n = w.shape[-1]
    g, tg, _ = x3.shape
    gb = tm // tg
    nt = m // tm
    te = jnp.zeros((nt,), jnp.int32)
    tv = jnp.ones((nt,), jnp.int32)
    grid_spec = pltpu.PrefetchScalarGridSpec(
        num_scalar_prefetch=2,
        grid=(nt, n // tn, k // tk),
        in_specs=[pl.BlockSpec((tm, tk), lambda i, j, kk, te, tv: (i, kk)),
                  pl.BlockSpec((1, tk, tn), lambda i, j, kk, te, tv: (te[i], kk, j)),
                  pl.BlockSpec((gb, tg, tn), lambda i, j, kk, te, tv: (i, 0, j)),
                  pl.BlockSpec((gb, 1, tn), lambda i, j, kk, te, tv: (i, 0, j))],
        out_specs=pl.BlockSpec((gb, tg, tn), lambda i, j, kk, te, tv: (i, 0, j)),
        scratch_shapes=[pltpu.VMEM((tm, tn), F32)],
    )
    return pl.pallas_call(
        _kt_residual_kernel,
        grid_spec=grid_spec,
        out_shape=jax.ShapeDtypeStruct(x3.shape, F32),
        compiler_params=_params("arbitrary", "arbitrary", "arbitrary"),
        name=name,
    )(te, tv, a, w.reshape((1,) + w.shape[-2:]), x3, gate3)


def _expert_down_call(a, w, rowscale, items, tr, nsub, tn, tk, name):
    m, k = a.shape
    n = w.shape[-1]
    tb = tr * nsub
    nk = k // tk
    n_items = items[0].shape[0]

    def kk_of(it, kk, pf):
        return jnp.where(pf[4][it] != 0, kk, nk - 1)

    grid_spec = pltpu.PrefetchScalarGridSpec(
        num_scalar_prefetch=7,
        grid=(n // tn, n_items, nk),
        in_specs=[pl.BlockSpec((tb, tk), lambda j, it, kk, *pf: (pf[0][it], kk_of(it, kk, pf))),
                  pl.BlockSpec((1, tk, tn), lambda j, it, kk, *pf: (pf[1][it], kk_of(it, kk, pf), j)),
                  pl.BlockSpec((1, tb, 1), lambda j, it, kk, *pf: (pf[0][it], 0, 0))],
        out_specs=pl.BlockSpec((tb, tn), lambda j, it, kk, *pf: (pf[0][it], j)),
        scratch_shapes=[pltpu.VMEM((tb, tn), F32)],
    )
    return pl.pallas_call(
        functools.partial(_expert_down_kernel, nsub=nsub, tr=tr),
        grid_spec=grid_spec,
        out_shape=jax.ShapeDtypeStruct((m, n), F32),
        compiler_params=_params("arbitrary", "arbitrary", "arbitrary"),
        name=name,
    )(*items, a, w, rowscale.reshape(m // tb, tb, 1))


def _conv_kernel(x_ref, prev_ref, w_ref, b_ref, o_ref, xp_ref, *, tl, taps):
    t = pl.program_id(2)
    hist = taps - 1
    top = SUBLANES - hist

    @pl.when(t == 0)
    def _():
        xp_ref[top:SUBLANES, :] = prev_ref[0]

    @pl.when(t > 0)
    def _():
        xp_ref[0:SUBLANES, :] = xp_ref[tl:tl + SUBLANES, :]

    x = x_ref[...]
    xp_ref[SUBLANES:SUBLANES + tl, :] = x
    w = w_ref[...]
    acc = b_ref[...] + w[hist:taps] * x
    for k in range(hist):
        acc = acc + w[k:k + 1] * xp_ref[top + k:top + k + tl, :]
    o_ref[...] = _silu(acc)


def _conv_call(proj, row0, nb, seq, col0, conv_prev, conv_w, conv_b, name):
    taps, c = conv_w.shape
    tl = _tile(seq, 512, SUBLANES)
    tc = _tile(math.gcd(c, col0) if col0 else c, 1024)
    nt = seq // tl
    assert row0 % tl == 0 and col0 % tc == 0
    rb0, cb0 = row0 // tl, col0 // tc
    return pl.pallas_call(
        functools.partial(_conv_kernel, tl=tl, taps=taps),
        grid=(nb, c // tc, nt),
        in_specs=[pl.BlockSpec((tl, tc), lambda b, j, t: (rb0 + b * nt + t, cb0 + j)),
                  pl.BlockSpec((1, taps - 1, tc), lambda b, j, t: (b, 0, j)),
                  pl.BlockSpec((taps, tc), lambda b, j, t: (0, j)),
                  pl.BlockSpec((1, tc), lambda b, j, t: (0, j))],
        out_specs=pl.BlockSpec((tl, tc), lambda b, j, t: (b * nt + t, j)),
        out_shape=jax.ShapeDtypeStruct((nb * seq, c), F32),
        scratch_shapes=[pltpu.VMEM((tl + SUBLANES, tc), F32)],
        compiler_params=_params("arbitrary", "arbitrary", "arbitrary"),
        name=name,
    )(proj, conv_prev, conv_w, conv_b.reshape(1, c))


def _ssd_kernel(x_ref, b_ref, c_ref, dtr_ref, z_ref, sel_ref, e_ref, bias_ref, alog_ref, dskip_ref, nw_ref,
                s0_ref, g_ref, s_ref, st_ref, *, q, heads, hdim, gpb):
    c = pl.program_id(2)
    rp = heads * hdim
    n = b_ref.shape[1] // gpb
    per_lane = LANES // hdim

    @pl.when(c == 0)
    def _():
        for k in range(gpb):
            st_ref[k] = s0_ref[0, k].T

    dt_all = _softplus(dtr_ref[...] + bias_ref[...])
    da_all = dt_all * -jnp.exp(alog_ref[...])
    ri = lax.broadcasted_iota(jnp.int32, (q, q), 0)
    ci = lax.broadcasted_iota(jnp.int32, (q, q), 1)
    causal = ri >= ci
    tril = jnp.where(causal, 1.0, 0.0).astype(BF16)
    expand = e_ref[...]
    lane = lax.broadcasted_iota(jnp.int32, (q, LANES), 1)

    for k in range(gpb):
        cols = slice(k * rp, (k + 1) * rp)
        sel = sel_ref[k]
        dtg = _sel_dot(dt_all, sel)
        dag = _sel_dot(da_all, sel)
        acs = _sel_dot_left(tril, dag)
        acs_t = acs.T
        dt_exp = _sel_dot(dtg, expand)
        acs_exp = _sel_dot(acs, expand)

        x = x_ref[:, cols]
        xdt = x * dt_exp
        b_blk = b_ref[:, k * n:(k + 1) * n]
        bm = b_blk.astype(BF16)
        cm = c_ref[:, k * n:(k + 1) * n].astype(BF16)
        cb = _dot_nt(cm, bm)

        st = st_ref[k]
        y = _dot(cm, st.astype(BF16)) * jnp.exp(acs_exp)
        last = acs_exp[q - 1:q, :]
        xs = (xdt * jnp.exp(last - acs_exp)).astype(BF16)
        st_ref[k] = st * jnp.exp(last) + _dot(b_blk.T.astype(BF16), xs)

        pieces = []
        for pr in range(heads // per_lane):
            xp = xdt[:, pr * LANES:(pr + 1) * LANES]
            acc = None
            for hh in range(per_lane):
                r = pr * per_lane + hh
                seg = acs[:, r:r + 1] - acs_t[r:r + 1, :]
                dec = jnp.exp(jnp.where(causal, seg, -jnp.inf))
                m = (cb * dec).astype(BF16)
                in_head = jnp.logical_and(lane >= hh * hdim, lane < (hh + 1) * hdim)
                xm = jnp.where(in_head, xp, 0.0).astype(BF16)
                t = _dot(m, xm)
                acc = t if acc is None else acc + t
            pieces.append(acc)
        y = y + jnp.concatenate(pieces, axis=1) + dskip_ref[:, cols] * x

        gt = y * _silu(z_ref[:, cols])
        ms = jnp.mean(gt * gt, axis=-1, keepdims=True)
        g_ref[:, cols] = (gt * lax.rsqrt(ms + EPS) * nw_ref[:, cols]).astype(g_ref.dtype)

    @pl.when(c == pl.num_programs(2) - 1)
    def _():
        for k in range(gpb):
            s_ref[0, k] = st_ref[k].T


def _ssd_call(xbc, proj, dt_raw, row0, nb, seq, s0, consts, name):
    sel, expand, bias, alog, dskip, nw = consts
    groups, n_heads, _ = sel.shape
    _, _, rp, n = s0.shape
    heads = n_heads // groups
    hdim = rp // heads
    inner = groups * rp
    gpb = 4 if groups % 4 == 0 else (2 if groups % 2 == 0 else 1)
    q = SSD_CHUNK if seq % SSD_CHUNK == 0 else seq
    nc = seq // q
    assert row0 % q == 0 and LANES % hdim == 0 and heads % (LANES // hdim) == 0 and heads <= LANES
    assert (inner // n) % gpb == 0
    rb0 = row0 // q
    b_col0 = inner // (gpb * n)
    c_col0 = (inner + groups * n) // (gpb * n)
    return pl.pallas_call(
        functools.partial(_ssd_kernel, q=q, heads=heads, hdim=hdim, gpb=gpb),
        grid=(nb, groups // gpb, nc),
        in_specs=[pl.BlockSpec((q, gpb * rp), lambda b, g, c: (b * nc + c, g)),
                  pl.BlockSpec((q, gpb * n), lambda b, g, c: (b * nc + c, b_col0 + g)),
                  pl.BlockSpec((q, gpb * n), lambda b, g, c: (b * nc + c, c_col0 + g)),
                  pl.BlockSpec((q, n_heads), lambda b, g, c: (rb0 + b * nc + c, 0)),
                  pl.BlockSpec((q, gpb * rp), lambda b, g, c: (rb0 + b * nc + c, g)),
                  pl.BlockSpec((gpb, n_heads, LANES), lambda b, g, c: (g, 0, 0)),
                  pl.BlockSpec((LANES, rp), lambda b, g, c: (0, 0)),
                  pl.BlockSpec((1, n_heads), lambda b, g, c: (0, 0)),
                  pl.BlockSpec((1, n_heads), lambda b, g, c: (0, 0)),
                  pl.BlockSpec((1, gpb * rp), lambda b, g, c: (0, g)),
                  pl.BlockSpec((1, gpb * rp), lambda b, g, c: (0, g)),
                  pl.BlockSpec((1, gpb, rp, n), lambda b, g, c: (b, g, 0, 0))],
        out_specs=[pl.BlockSpec((q, gpb * rp), lambda b, g, c: (b * nc + c, g)),
                   pl.BlockSpec((1, gpb, rp, n), lambda b, g, c: (b, g, 0, 0))],
        out_shape=[jax.ShapeDtypeStruct((nb * seq, inner), BF16),
                   jax.ShapeDtypeStruct(s0.shape, F32)],
        scratch_shapes=[pltpu.VMEM((gpb, n, rp), F32)],
        compiler_params=_params("arbitrary", "arbitrary", "arbitrary"),
        name=name,
    )(xbc, xbc, xbc, dt_raw, proj, sel, expand, bias, alog, dskip, nw, s0)


def _sb_weights(z, carry, suffix, strict_mask):
    sub = suffix.shape[0]
    tail = jnp.log(1.0 + jnp.exp(-jnp.abs(z)))
    log_beta = jnp.minimum(z, 0.0) - tail
    log_keep = jnp.minimum(-z, 0.0) - tail
    if strict_mask is not None:
        log_keep = jnp.where(strict_mask, log_keep, 0.0)
    nsub = z.shape[1] // sub
    later = [None] * nsub
    run = carry
    for j in reversed(range(nsub)):
        lk = log_keep[:, j * sub:(j + 1) * sub]
        later[j] = _sel_dot(lk, suffix, parts=2) + run
        run = run + jnp.sum(lk, axis=1, keepdims=True)
    later = later[0] if nsub == 1 else jnp.concatenate(later, axis=1)
    w = jnp.exp(log_beta + later)
    if strict_mask is not None:
        w = jnp.where(strict_mask, w, 0.0)
    return w, run


def _suffix_matrix(n):
    ri = lax.broadcasted_iota(jnp.int32, (n, n), 0)
    ci = lax.broadcasted_iota(jnp.int32, (n, n), 1)
    return jnp.where(ri > ci, 1.0, 0.0).astype(BF16)


def _stacked_strict_mask(nh, t):
    ri = lax.broadcasted_iota(jnp.int32, (nh * t, t), 0)
    ci = lax.broadcasted_iota(jnp.int32, (nh * t, t), 1)
    return ci < jnp.bitwise_and(ri, t - 1)


def _sb_scores(q_ref, load_k, nh, hd, scale, h0=0):
    zs = [_dot_nt(q_ref[:, (h0 + h) * hd:(h0 + h + 1) * hd], load_k(h0 + h)) for h in range(nh)]
    return jnp.concatenate(zs, axis=0) * scale


def _sb_values(w, load_v, nh, t, h0=0):
    wb = w.astype(BF16)
    return jnp.concatenate([_dot(wb[h * t:(h + 1) * t], load_v(h0 + h)) for h in range(nh)], axis=0)


def _sb_prompt_kernel(q_ref, k_ref, v_ref, o_ref, acc_ref, carry_ref, *, blk, nh, hd, scale):
    qi = pl.program_id(2)
    suffix = _suffix_matrix(blk)

    def visit(kb, carry, mask):
        ks = pl.multiple_of(kb * blk, blk)
        z = _sb_scores(q_ref, lambda h: k_ref[pl.ds(ks, blk), h * hd:(h + 1) * hd], nh, hd, scale)
        w, run = _sb_weights(z, carry, suffix, mask)
        return _sb_values(w, lambda h: v_ref[pl.ds(ks, blk), h * hd:(h + 1) * hd], nh, blk), run

    out, run = visit(qi, jnp.zeros((nh * blk, 1), F32), _stacked_strict_mask(nh, blk))
    acc_ref[...] = out
    carry_ref[...] = run

    def body(i, c):
        out, run = visit(qi - 1 - i, carry_ref[...], None)
        acc_ref[...] += out
        carry_ref[...] = run
        return c

    lax.fori_loop(0, qi, body, 0)
    for h in range(nh):
        o_ref[:, h * hd:(h + 1) * hd] = acc_ref[h * blk:(h + 1) * blk, :].astype(o_ref.dtype)


def _heads_per_step(n_heads):
    return SUBLANES if n_heads % SUBLANES == 0 else n_heads


def _sb_prompt_call(q16, kv16, row0, nb, seq, n_heads, hd, scale, name):
    blk = SB_BLOCK
    nq = seq // blk
    nh = _heads_per_step(n_heads)
    ng = n_heads // nh
    assert row0 % seq == 0 and hd % LANES == 0 and blk & (blk - 1) == 0
    sb0 = row0 // seq
    rb0 = row0 // blk
    return pl.pallas_call(
        functools.partial(_sb_prompt_kernel, blk=blk, nh=nh, hd=hd, scale=scale),
        grid=(nb, ng, nq),
        in_specs=[pl.BlockSpec((blk, nh * hd), lambda b, g, i: (rb0 + b * nq + i, g)),
                  pl.BlockSpec((seq, nh * hd), lambda b, g, i: (sb0 + b, g)),
                  pl.BlockSpec((seq, nh * hd), lambda b, g, i: (sb0 + b, ng + g))],
        out_specs=pl.BlockSpec((blk, nh * hd), lambda b, g, i: (b * nq + i, g)),
        out_shape=jax.ShapeDtypeStruct((nb * seq, n_heads * hd), BF16),
        scratch_shapes=[pltpu.VMEM((nh * blk, hd), F32), pltpu.VMEM((nh * blk, 1), F32)],
        compiler_params=_params("arbitrary", "arbitrary", "arbitrary"),
        name=name,
    )(q16, kv16, kv16)


def _sb_decode_kernel(q_ref, kn_ref, vn_ref, kc_ref, vc_ref, o_ref, acc_ref, carry_ref, *, nh, hd, sub, scale):
    p = pl.program_id(2)
    t = q_ref.shape[0]

    @pl.when(p == 0)
    def _():
        z = _sb_scores(q_ref, lambda h: kn_ref[:, h * hd:(h + 1) * hd], nh, hd, scale)
        w, run = _sb_weights(z, jnp.zeros((nh * t, 1), F32), _suffix_matrix(t), _stacked_strict_mask(nh, t))
        acc_ref[...] = _sb_values(w, lambda h: vn_ref[:, h * hd:(h + 1) * hd], nh, t)
        carry_ref[...] = run

    pb = kc_ref.shape[0]
    k_rows = kc_ref.reshape(pb * nh, hd)
    v_rows = vc_ref.reshape(pb * nh, hd)
    z = _sb_scores(q_ref, lambda h: k_rows[pl.ds(h, pb, stride=nh), :].astype(BF16), nh, hd, scale)
    w, run = _sb_weights(z, carry_ref[...], _suffix_matrix(sub), None)
    acc_ref[...] += _sb_values(w, lambda h: v_rows[pl.ds(h, pb, stride=nh), :].astype(BF16), nh, t)
    carry_ref[...] = run

    @pl.when(p == pl.num_programs(2) - 1)
    def _():
        for h in range(nh):
            o_ref[:, h * hd:(h + 1) * hd] = acc_ref[h * t:(h + 1) * t, :].astype(o_ref.dtype)


def _sb_decode_call(q16, kv16, row0, nb, seq, n_heads, hd, cache_k, cache_v, scale, name):
    past = cache_k.shape[1]
    nh = _heads_per_step(n_heads)
    ng = n_heads // nh
    pb = _tile(past, 512)
    sub = _tile(pb, 256)
    npb = past // pb
    assert row0 % seq == 0 and seq & (seq - 1) == 0
    sb0 = row0 // seq
    return pl.pallas_call(
        functools.partial(_sb_decode_kernel, nh=nh, hd=hd, sub=sub, scale=scale),
        grid=(nb, ng, npb),
        in_specs=[pl.BlockSpec((seq, nh * hd), lambda b, g, p: (sb0 + b, g)),
                  pl.BlockSpec((seq, nh * hd), lambda b, g, p: (sb0 + b, g)),
                  pl.BlockSpec((seq, nh * hd), lambda b, g, p: (sb0 + b, ng + g)),
                  pl.BlockSpec((None, pb, nh, hd), lambda b, g, p: (b, npb - 1 - p, g, 0)),
                  pl.BlockSpec((None, pb, nh, hd), lambda b, g, p: (b, npb - 1 - p, g, 0))],
        out_specs=pl.BlockSpec((seq, nh * hd), lambda b, g, p: (b, g)),
        out_shape=jax.ShapeDtypeStruct((nb * seq, n_heads * hd), BF16),
        scratch_shapes=[pltpu.VMEM((nh * seq, hd), F32), pltpu.VMEM((nh * seq, 1), F32)],
        compiler_params=_params("arbitrary", "arbitrary", "arbitrary"),
        name=name,
    )(q16, kv16, kv16, cache_k, cache_v)


def _gather_cast_kernel(src_ref, h_ref, o_ref, buf_ref, sem):
    tm = buf_ref.shape[0]

    def row_copy(r):
        return pltpu.make_async_copy(h_ref.at[pl.ds(src_ref[0, 0, r], 1), :], buf_ref.at[pl.ds(r, 1), :], sem)

    def start(r2, carry):
        for thread in range(2):
            row_copy(2 * r2 + thread).start(priority=thread)
        return carry

    def wait(r, carry):
        row_copy(r).wait()
        return carry

    lax.fori_loop(0, tm // 2, start, 0)
    lax.fori_loop(0, tm, wait, 0)
    o_ref[...] = buf_ref[...].astype(o_ref.dtype)


def _gather_cast_call(h, src_rows, tm):
    m, d = h.shape
    nt = src_rows.shape[0] // tm
    return pl.pallas_call(
        _gather_cast_kernel,
        grid=(nt,),
        in_specs=[pl.BlockSpec((1, 1, tm), lambda i: (i, 0, 0), memory_space=pltpu.SMEM),
                  pl.BlockSpec(memory_space=pl.ANY)],
        out_specs=pl.BlockSpec((tm, d), lambda i: (i, 0)),
        out_shape=jax.ShapeDtypeStruct((nt * tm, d), BF16),
        scratch_shapes=[pltpu.VMEM((tm, d), F32), pltpu.SemaphoreType.DMA(())],
        compiler_params=_params("arbitrary"),
        name="moe_gather_rows",
    )(src_rows.reshape(nt, 1, tm), h)


def _combine_kernel(p1_ref, p2_ref, ys_ref, x_ref, g_ref, o_ref, buf_ref, sem):
    tm = buf_ref.shape[1]

    def copies(r):
        return (pltpu.make_async_copy(ys_ref.at[pl.ds(p1_ref[0, 0, r], 1), :], buf_ref.at[0, pl.ds(r, 1), :], sem),
                pltpu.make_async_copy(ys_ref.at[pl.ds(p2_ref[0, 0, r], 1), :], buf_ref.at[1, pl.ds(r, 1), :], sem))

    def start(r, carry):
        for thread, cp in enumerate(copies(r)):
            cp.start(priority=thread)
        return carry

    def wait(r, carry):
        for cp in copies(r):
            cp.wait()
        return carry

    lax.fori_loop(0, tm, start, 0)
    lax.fori_loop(0, tm, wait, 0)
    f = (buf_ref[0] + buf_ref[1]).reshape(o_ref.shape)
    o_ref[...] = x_ref[...] + g_ref[...] * f


def _combine_call(ys, pos1, pos2, x3, gate3, tm):
    g, tg, d = x3.shape
    gb = tm // tg
    nt = (g * tg) // tm
    return pl.pallas_call(
        _combine_kernel,
        grid=(nt,),
        in_specs=[pl.BlockSpec((1, 1, tm), lambda i: (i, 0, 0), memory_space=pltpu.SMEM),
                  pl.BlockSpec((1, 1, tm), lambda i: (i, 0, 0), memory_space=pltpu.SMEM),
                  pl.BlockSpec(memory_space=pl.ANY),
                  pl.BlockSpec((gb, tg, d), lambda i: (i, 0, 0)),
                  pl.BlockSpec((gb, 1, d), lambda i: (i, 0, 0))],
        out_specs=pl.BlockSpec((gb, tg, d), lambda i: (i, 0, 0)),
        out_shape=jax.ShapeDtypeStruct(x3.shape, F32),
        scratch_shapes=[pltpu.VMEM((2, tm, d), F32), pltpu.SemaphoreType.DMA(())],
        compiler_params=_params("arbitrary"),
        name="moe_combine_rows",
    )(pos1.reshape(nt, 1, tm), pos2.reshape(nt, 1, tm), ys, x3, gate3)


def _final_norm_kernel(x_ref, w_ref, o_ref):
    x = x_ref[...]
    ms = jnp.mean(x * x, axis=-1, keepdims=True)
    o_ref[...] = x * lax.rsqrt(ms + EPS) * w_ref[...]


def _final_norm_call(x2, w, row0, n_rows, name):
    _, d = x2.shape
    tm = _tile(math.gcd(n_rows, row0) if row0 else n_rows, 256, SUBLANES)
    rb0 = row0 // tm
    return pl.pallas_call(
        _final_norm_kernel,
        grid=(n_rows // tm,),
        in_specs=[pl.BlockSpec((tm, d), lambda i: (rb0 + i, 0)), pl.BlockSpec((1, d), lambda i: (0, 0))],
        out_specs=pl.BlockSpec((tm, d), lambda i: (i, 0)),
        out_shape=jax.ShapeDtypeStruct((n_rows, d), F32),
        compiler_params=_params("arbitrary"),
        name=name,
    )(x2, w.reshape(1, d))


def _ssd_constants(groups, n_heads, hdim, dt_bias, a_log, d_skip, norm_w):
    heads = n_heads // groups
    sel = np.zeros((groups, n_heads, LANES), np.float32)
    for g in range(groups):
        for r in range(heads):
            sel[g, g * heads + r, r] = 1.0
    expand = np.zeros((LANES, heads * hdim), np.float32)
    for r in range(heads):
        expand[r, r * hdim:(r + 1) * hdim] = 1.0
    return (jnp.asarray(sel, BF16), jnp.asarray(expand, BF16),
            dt_bias.astype(F32).reshape(1, n_heads), a_log.astype(F32).reshape(1, n_heads),
            jnp.repeat(d_skip.astype(F32), hdim).reshape(1, n_heads * hdim), norm_w.reshape(1, n_heads * hdim))


def _route(route, n_experts, tr, nsub):
    m = route.shape[0]
    idx = route[:, :TOP_K].astype(jnp.int32)
    gates = route[:, TOP_K:2 * TOP_K]
    flat_e = idx.reshape(-1)
    n_assign = flat_e.shape[0]
    n_tiles = -(-(n_assign // tr + n_experts) // nsub) * nsub
    n_blocks = n_tiles // nsub
    order = jnp.argsort(flat_e, stable=True)
    rank = jnp.argsort(order).astype(jnp.int32)
    counts = jnp.sum((flat_e[:, None] == jnp.arange(n_experts, dtype=jnp.int32)[None, :]).astype(jnp.int32), axis=0)
    tiles = (counts + tr - 1) // tr
    padded = tiles * tr
    pad_end = jnp.cumsum(padded)
    pad_start = pad_end - padded
    cnt_start = jnp.cumsum(counts) - counts
    dest2 = (pad_start[flat_e] + rank - cnt_start[flat_e]).reshape(m, TOP_K)

    t_idx = jnp.arange(n_tiles, dtype=jnp.int32)
    t_start = t_idx * tr
    sub_e = jnp.minimum(jnp.searchsorted(pad_end, t_start, side="right"), n_experts - 1).astype(jnp.int32)
    sub_v = t_start < pad_end[-1]
    row = jnp.arange(n_tiles * tr, dtype=jnp.int32)
    row_e = jnp.repeat(sub_e, tr)
    within = row - pad_start[row_e]
    holds = within < counts[row_e]
    assign = order[jnp.clip(cnt_start[row_e] + within, 0, n_assign - 1)]
    src_rows = jnp.where(holds, assign // TOP_K, 0).astype(jnp.int32)
    row_gate = jnp.where(holds, gates.reshape(-1)[assign], 0.0)
    prev_e = jnp.concatenate([jnp.full((1,), -1, jnp.int32), sub_e[:-1]])
    run_start = jnp.logical_or(t_idx % nsub == 0, jnp.logical_and(sub_v, sub_e != prev_e))
    n_items = n_blocks + n_experts - 1
    item_sub = jnp.nonzero(run_start, size=n_items, fill_value=-1)[0].astype(jnp.int32)
    item_exists = item_sub >= 0
    n_exist = jnp.sum(item_exists.astype(jnp.int32))
    item_sub = jnp.where(item_exists, item_sub, item_sub[jnp.maximum(n_exist - 1, 0)])
    item_live = jnp.logical_and(item_exists, sub_v[item_sub])
    item_blk = item_sub // nsub
    item_exp = sub_e[item_sub]
    item_first = jnp.logical_and(item_exists, item_sub % nsub == 0)
    next_blk = jnp.concatenate([item_blk[1:], jnp.full((1,), -1, jnp.int32)])
    next_exists = jnp.concatenate([item_exists[1:], jnp.zeros((1,), bool)])
    item_last = jnp.logical_and(item_exists, jnp.logical_or(jnp.logical_not(next_exists), next_blk != item_blk))

    def i32(v):
        return v.astype(jnp.int32)

    items = (i32(item_blk), i32(item_exp), i32(item_first), i32(item_last), i32(item_live), sub_e, i32(sub_v))
    return src_rows, row_gate, i32(pad_start), i32(tiles), items, dest2[:, 0], dest2[:, 1]


def kernel(x_prompt, x_sample, state_ssm, state_conv, cache_k, cache_v, c_prompt, c_sample, w_mod, b_mod, ssd_w_in, ssd_conv_w, ssd_conv_b, ssd_dt_bias, ssd_a_log, ssd_d, ssd_norm_w, ssd_w_out, sb_w_qkv, sb_w_o, ffn_w_gate, ffn_w_up, ffn_w_down, moe_w_router, moe_w_gate, moe_w_up, moe_w_down, final_norm_w):
    bp, lp, d = x_prompt.shape
    bs, ls, _ = x_sample.shape
    depth = w_mod.shape[0]
    mp, msamp = bp * lp, bs * ls
    m = mp + msamp
    tg = math.gcd(lp, ls)
    assert tg % SUBLANES == 0
    n_groups = m // tg
    n_heads = ssd_a_log.shape[1]
    hdim, n_state = state_ssm.shape[3], state_ssm.shape[4]
    inner = n_heads * hdim
    conv_dim = ssd_conv_w.shape[2]
    ssd_groups = (conv_dim - inner) // (2 * n_state)
    sb_heads, sb_hd = cache_k.shape[3], cache_k.shape[4]
    n_experts = moe_w_router.shape[2]
    d_ff = ffn_w_gate.shape[2]
    tm = _tile(m, 1024, tg)
    tm_res = _tile(m, 1536, tg)

    seq_of_group = np.concatenate([np.repeat(np.arange(bp), lp // tg), bp + np.repeat(np.arange(bs), ls // tg)])
    n_seq = bp + bs
    rows = -(-n_seq // 16) * 16
    c_all = jnp.zeros((rows, d), F32).at[:n_seq].set(jnp.concatenate([c_prompt, c_sample], axis=0))
    mod = _mod_call(c_all, w_mod, b_mod)
    mod_g = mod[:, seq_of_group, :].reshape(depth, n_groups, 1, 6, d)

    def mvec(i, which):
        return mod_g[i, :, :, which, :]

    x3 = jnp.concatenate([x_prompt.reshape(mp, d), x_sample.reshape(msamp, d)], axis=0).reshape(n_groups, tg, d)
    zeros_ssm = jnp.zeros((bp,) + state_ssm.shape[2:], F32)
    zeros_conv = jnp.zeros((bp,) + state_conv.shape[2:], F32)
    ssm_p, conv_p, k_p, v_p, ssm_s, conv_s, k_s, v_s = [], [], [], [], [], [], [], []

    for i in range(depth):
        j = i // 2
        h = _modulate_call(x3, mvec(i, 0), mvec(i, 1), BF16).reshape(m, d)
        if i % 2 == 0:
            w_in = ssd_w_in[j]
            n_main = inner + conv_dim
            proj, = _matmul_ws(h, w_in, 0, n_main, _tile(n_main, 512), tm, "ssd_in_proj")
            dt_raw, = _matmul_ws(h, w_in, n_main, n_heads, n_heads, tm, "ssd_in_proj_dt")
            consts = _ssd_constants(ssd_groups, n_heads, hdim, ssd_dt_bias[j], ssd_a_log[j], ssd_d[j], ssd_norm_w[j])
            outs = []
            for (row0, nb, seq, prev, s0, conv_out, ssm_out, tag) in (
                    (0, bp, lp, zeros_conv, zeros_ssm, conv_p, ssm_p, "prompt"),
                    (mp, bs, ls, state_conv[j], state_ssm[j], conv_s, ssm_s, "sample")):
                xbc = _conv_call(proj, row0, nb, seq, inner, prev, ssd_conv_w[j], ssd_conv_b[j], "ssd_conv_" + tag)
                s0g = s0.astype(F32).reshape(nb, ssd_groups, (n_heads // ssd_groups) * hdim, n_state)
                g_out, s_fin = _ssd_call(xbc, proj, dt_raw, row0, nb, seq, s0g, consts, "ssd_scan_" + tag)
                outs.append(g_out)
                ssm_out.append(s_fin.reshape(nb, n_heads, hdim, n_state))
                hist = ssd_conv_w.shape[1] - 1
                last_rows = (row0 + np.arange(nb)[:, None] * seq + np.arange(seq - hist, seq)[None, :]).reshape(-1)
                conv_out.append(jnp.take(proj, last_rows, axis=0)[:, inner:].reshape(nb, hist, conv_dim))
            mix = jnp.concatenate(outs, axis=0)
            w_mix = ssd_w_out[j]
        else:
            hd_all = sb_heads * sb_hd
            tn = _tile(hd_all, 512)
            q16, = _matmul_ws(h, sb_w_qkv[j], 0, hd_all, tn, tm, "sb_q", (BF16,))
            kv32, kv16 = _matmul_ws(h, sb_w_qkv[j], hd_all, 2 * hd_all, tn, tm, "sb_kv", (F32, BF16))
            scale = float(sb_hd) ** -0.5
            o_p = _sb_prompt_call(q16, kv16, 0, bp, lp, sb_heads, sb_hd, scale, "sb_attn_prompt")
            o_s = _sb_decode_call(q16, kv16, mp, bs, ls, sb_heads, sb_hd, cache_k[j], cache_v[j], scale,
                                  "sb_attn_sample")
            mix = jnp.concatenate([o_p, o_s], axis=0)
            w_mix = sb_w_o[j]
            for (row0, nb, seq, k_out, v_out) in ((0, bp, lp, k_p, v_p), (mp, bs, ls, k_s, v_s)):
                blk = kv32[row0:row0 + nb * seq]
                k_out.append(blk[:, :hd_all].reshape(nb, seq, sb_heads, sb_hd))
                v_out.append(blk[:, hd_all:].reshape(nb, seq, sb_heads, sb_hd))
        kmix = mix.shape[1]
        x3 = _matmul_residual(mix, w_mix, x3, mvec(i, 2), tm_res, _tile(d, 1024), _tile(kmix, 1024), "mixer_out_proj")

        if i % 2 == 0:
            h = _modulate_call(x3, mvec(i, 3), mvec(i, 4), BF16).reshape(m, d)
            nt = m // tm
            act = _swiglu_call(h, ffn_w_gate[j][None], ffn_w_up[j][None], jnp.zeros((nt,), jnp.int32),
                               jnp.ones((nt,), jnp.int32), tm, _tile(d_ff, 256), "ffn_up")
            x3 = _matmul_residual(act, ffn_w_down[j], x3, mvec(i, 5), tm_res, _tile(d, 1024), _tile(d_ff, 1024), "ffn_down")
        else:
            h32, route = _modulate_router_call(x3, mvec(i, 3), mvec(i, 4), moe_w_router[j])
            tr = _tile(m, 256, tg)
            nsub = 4
            src_rows, row_gate, group_start, group_tiles, items, pos1, pos2 = _route(
                route.reshape(m, LANES), n_experts, tr, nsub)
            hs = _gather_cast_call(h32.reshape(m, d), src_rows, nsub * tr // 2)
            act = _expert_up_call(hs, moe_w_gate[j], moe_w_up[j], group_start, group_tiles, tr, _tile(d_ff, 512),
                                  "moe_up")
            ys = _expert_down_call(act, moe_w_down[j], row_gate, items, tr, nsub, _tile(d, 2048), _tile(d_ff, 1024),
                                   "moe_down")
            x3 = _combine_call(ys, pos1, pos2, x3, mvec(i, 5), _tile(m, 256, tg))

    x2 = x3.reshape(m, d)
    y_prompt = _final_norm_call(x2, final_norm_w, 0, mp, "final_norm_prompt").reshape(bp, lp, d)
    y_sample = _final_norm_call(x2, final_norm_w, mp, msamp, "final_norm_sample").reshape(bs, ls, d)
    return (y_prompt, y_sample, jnp.stack(ssm_p), jnp.stack(conv_p), jnp.stack(k_p), jnp.stack(v_p),
            jnp.stack(ssm_s), jnp.stack(conv_s), jnp.stack(k_s), jnp.stack(v_s))
```
